```python
import math
import jax
import jax.numpy as jnp
from jax import lax
import numpy as np

D_MODEL = 1024
BATCH = 32
SEQ = 2048
DEPTH = 1
DEC_BATCH = 8
DEC_SEQ = 16
PAST_LEN = 1024

CHUNK = 64
N_META = 16
META_CHUNK = -1
Q_BLOCK = 128
D_CONV = D_MODEL
CONV_W = 3
N_HEADS = 8
D_QK = 64
D_V = 2 * D_QK
D_ATTN = N_HEADS * D_V
PROJ_SIZES = (D_CONV, D_CONV, D_CONV, N_HEADS * 2 * D_QK, N_HEADS * 2 * D_QK, D_ATTN, D_MODEL, D_MODEL)
PROJ_SPLITS = (D_CONV, 2 * D_CONV, 3 * D_CONV, 3 * D_CONV + N_HEADS * 2 * D_QK, 3 * D_CONV + N_HEADS * 4 * D_QK, 3 * D_CONV + N_HEADS * 4 * D_QK + D_ATTN, 3 * D_CONV + N_HEADS * 4 * D_QK + D_ATTN + D_MODEL)
D_PROJ = 3 * D_CONV + N_HEADS * 4 * D_QK + D_ATTN + 2 * D_MODEL
N_GROUPS = 4
EXP_PER_GROUP = 8
N_EXPERTS = N_GROUPS * EXP_PER_GROUP
TOP_K_INNER = 2
D_EXPERT = D_MODEL // 4
ALPHA = (2.0 * DEPTH) ** 0.25
BETA = (8.0 * DEPTH) ** -0.25
LN_EPS = 1e-5
RMS_EPS = 1e-5

kernel_name = 'hybrid_conv_diffattn_hmoe_stream_step'


def layer_norm(x, g, b):
    xf = x.astype(jnp.float32)
    mu = jnp.mean(xf, axis=-1, keepdims=True)
    xc = xf - mu
    var = jnp.mean(xc * xc, axis=-1, keepdims=True)
    return (xc * lax.rsqrt(var + LN_EPS) * g + b).astype(x.dtype)


def frame_chunk_ids(start, n):
    return (start + jnp.arange(n, dtype=jnp.int32)) // CHUNK


def split_proj(p):
    h, bg, cg, q, k, v, gc, ga = jnp.split(p, PROJ_SPLITS, axis=-1)
    lead = p.shape[:-1]
    q = q.reshape(lead + (N_HEADS, 2, D_QK))
    k = k.reshape(lead + (N_HEADS, 2, D_QK))
    v = v.reshape(lead + (N_HEADS, D_V))
    return h, bg, cg, q, k, v, gc, ga


def causal_conv3(u_hist, w):
    L = u_hist.shape[1] - (CONV_W - 1)
    return sum(w[i] * u_hist[:, i:i + L] for i in range(CONV_W))


def conv_branch(h, bg, cg, u_prev, w):
    u = cg * h
    u_hist = jnp.concatenate([u_prev.astype(u.dtype), u], axis=1)
    return bg * causal_conv3(u_hist, w), u_hist[:, -(CONV_W - 1):]


def diff_lambda(lam_qk, lam_init):
    lf = lam_qk.astype(jnp.float32)
    return jnp.exp(jnp.sum(lf[0] * lf[1])) - jnp.exp(jnp.sum(lf[2] * lf[3])) + lam_init


def diff_attention(q, k, v, q_chunk, k_chunk, lam, lam_init, subln_g):
    s = jnp.einsum('bqhmd,bkhmd->bhmqk', q, k).astype(jnp.float32) * (D_QK ** -0.5)
    mask = k_chunk[None, :] <= q_chunk[:, None]
    p = jax.nn.softmax(jnp.where(mask, s, -jnp.inf), axis=-1)
    a = p[:, :, 0] - lam * p[:, :, 1]
    o = jnp.einsum('bhqk,bkhd->bqhd', a.astype(v.dtype), v).astype(jnp.float32)
    o = o * lax.rsqrt(jnp.mean(o * o, axis=-1, keepdims=True) + RMS_EPS) * subln_g * (1.0 - lam_init)
    return o.astype(v.dtype)


def prompt_attention(q, k, v, lam, lam_init, subln_g):
    B, L = q.shape[0], q.shape[1]
    n = L - N_META
    nb = n // Q_BLOCK
    cid = jnp.concatenate([jnp.full((N_META,), META_CHUNK, jnp.int32), frame_chunk_ids(0, n)])
    o_meta = diff_attention(q[:, :N_META], k, v, cid[:N_META], cid, lam, lam_init, subln_g)
    qf = jnp.moveaxis(q[:, N_META:].reshape(B, nb, Q_BLOCK, N_HEADS, 2, D_QK), 1, 0)
    cf = cid[N_META:].reshape(nb, Q_BLOCK)
    o_f = lax.map(lambda qc: diff_attention(qc[0], k, v, qc[1], cid, lam, lam_init, subln_g), (qf, cf))
    o_f = jnp.moveaxis(o_f, 0, 1).reshape(B, n, N_HEADS, D_V)
    return jnp.concatenate([o_meta, o_f], axis=1).reshape(B, L, D_ATTN)


def hier_moe(x, w_group, b_group, w_router, b_router, w_gate_up, w_down):
    T = x.shape[0]
    p_group = jax.nn.softmax((x @ w_group + b_group).astype(jnp.float32), axis=-1)
    p_g, g_idx = lax.top_k(p_group, 1)
    logits_e = (x @ w_router + b_router).astype(jnp.float32)
    in_group = g_idx * EXP_PER_GROUP + jnp.arange(EXP_PER_GROUP, dtype=jnp.int32)[None, :]
    top_vals, top_idx = lax.top_k(jnp.take_along_axis(logits_e, in_group, axis=1), TOP_K_INNER)
    weights = p_g * jax.nn.softmax(top_vals, axis=-1)
    expert_id = jnp.take_along_axis(in_group, top_idx, axis=1)
    gates = jnp.zeros((T, N_EXPERTS), jnp.float32).at[jnp.arange(T)[:, None], expert_id].add(weights)

    def expert_step(acc, ws):
        wgu, wd, g = ws
        a, b = jnp.split(x @ wgu, 2, axis=-1)
        return acc + g[:, None] * ((jax.nn.silu(a) * b) @ wd), None

    out, _ = lax.scan(expert_step, jnp.zeros_like(x), (w_gate_up, w_down, gates.T.astype(x.dtype)))
    return out


def layer_tail(x, yc, ya, gc, ga, w_branch, w_out, ln1_g, ln1_b, w_group, b_group, w_router, b_router, w_gate_up, w_down, ln2_g, ln2_b):
    merged = jax.nn.sigmoid(gc) * (yc @ w_branch[0]) + jax.nn.sigmoid(ga) * (ya @ w_branch[1])
    h = layer_norm(ALPHA * x + merged @ w_out, ln1_g, ln1_b)
    Bn, L, D = h.shape
    f = hier_moe(h.reshape(Bn * L, D), w_group, b_group, w_router, b_router, w_gate_up, w_down).reshape(Bn, L, D)
    return layer_norm(ALPHA * h + f, ln2_g, ln2_b)


def setup_inputs(seed: int = 0) -> dict:
    key = jax.random.key(seed)
    ks = jax.random.split(key, 24)
    nrm = jax.random.normal
    f32 = jnp.float32
    return {
        'x_prompt': nrm(ks[0], (BATCH, SEQ, D_MODEL), f32),
        'x_sample': nrm(ks[1], (DEC_BATCH, DEC_SEQ, D_MODEL), f32),
        'cache_k': nrm(ks[2], (DEPTH, DEC_BATCH, N_META + PAST_LEN, N_HEADS, 2 * D_QK), f32),
        'cache_v': nrm(ks[3], (DEPTH, DEC_BATCH, N_META + PAST_LEN, N_HEADS, D_V), f32),
        'state_conv': nrm(ks[4], (DEPTH, DEC_BATCH, CONV_W - 1, D_CONV), f32),
        'meta_tokens': nrm(ks[5], (N_META, D_MODEL), f32),
        'w_in': nrm(ks[6], (DEPTH, D_MODEL, D_PROJ), f32) * D_MODEL ** -0.5,
        'conv_w': nrm(ks[7], (DEPTH, CONV_W, D_CONV), f32) * CONV_W ** -0.5,
        'lambda_qk': nrm(ks[8], (DEPTH, 4, D_QK), f32) * 0.1,
        'subln_g': 1.0 + 0.02 * nrm(ks[9], (DEPTH, D_V), f32),
        'w_branch': nrm(ks[10], (DEPTH, 2, D_CONV, D_MODEL), f32) * D_CONV ** -0.5,
        'w_out': nrm(ks[11], (DEPTH, D_MODEL, D_MODEL), f32) * (D_MODEL ** -0.5 * BETA),
        'ln1_g': 1.0 + 0.02 * nrm(ks[12], (DEPTH, D_MODEL), f32),
        'ln1_b': 0.02 * nrm(ks[13], (DEPTH, D_MODEL), f32),
        'w_group': nrm(ks[14], (DEPTH, D_MODEL, N_GROUPS), f32) * D_MODEL ** -0.5,
        'b_group': 0.01 * nrm(ks[15], (DEPTH, N_GROUPS), f32),
        'w_router': nrm(ks[16], (DEPTH, D_MODEL, N_EXPERTS), f32) * D_MODEL ** -0.5,
        'b_router': 0.01 * nrm(ks[17], (DEPTH, N_EXPERTS), f32),
        'w_gate_up': nrm(ks[18], (DEPTH, N_EXPERTS, D_MODEL, 2 * D_EXPERT), f32) * D_MODEL ** -0.5,
        'w_down': nrm(ks[19], (DEPTH, N_EXPERTS, D_EXPERT, D_MODEL), f32) * (D_EXPERT ** -0.5 * BETA),
        'ln2_g': 1.0 + 0.02 * nrm(ks[20], (DEPTH, D_MODEL), f32),
        'ln2_b': 0.02 * nrm(ks[21], (DEPTH, D_MODEL), f32),
    }


def reference(x_prompt, x_sample, cache_k, cache_v, state_conv, meta_tokens, w_in, conv_w, lambda_qk, subln_g, w_branch, w_out, ln1_g, ln1_b, w_group, b_group, w_router, b_router, w_gate_up, w_down, ln2_g, ln2_b):
    Bp = x_prompt.shape[0]
    Bs, S = x_sample.shape[0], x_sample.shape[1]
    past = cache_k.shape[2] - N_META
    meta = jnp.broadcast_to(meta_tokens[None].astype(x_prompt.dtype), (Bp, N_META, D_MODEL))
    h_p = jnp.concatenate([meta, x_prompt], axis=1)
    h_s = x_sample
    cid_k = jnp.concatenate([jnp.full((N_META,), META_CHUNK, jnp.int32), frame_chunk_ids(0, past), frame_chunk_ids(past, S)])
    cid_q = frame_chunk_ids(past, S)
    nk_p, nv_p, nc_p, nk_s, nv_s, nc_s = [], [], [], [], [], []
    for l in range(DEPTH):
        lam_init = 0.8 - 0.6 * math.exp(-0.3 * l)
        lam = diff_lambda(lambda_qk[l], lam_init)
        hh, bg, cg, q, k, v, gc, ga = split_proj(h_p @ w_in[l])
        yc, conv_p = conv_branch(hh, bg, cg, jnp.zeros((Bp, CONV_W - 1, D_CONV), hh.dtype), conv_w[l])
        ya = prompt_attention(q, k, v, lam, lam_init, subln_g[l])
        nk_p.append(k.reshape(k.shape[:3] + (2 * D_QK,)))
        nv_p.append(v)
        nc_p.append(conv_p)
        sh, sbg, scg, sq, sk, sv, sgc, sga = split_proj(h_s @ w_in[l])
        syc, conv_s = conv_branch(sh, sbg, scg, state_conv[l], conv_w[l])
        k_all = jnp.concatenate([cache_k[l].reshape(Bs, N_META + past, N_HEADS, 2, D_QK).astype(sk.dtype), sk], axis=1)
        v_all = jnp.concatenate([cache_v[l].astype(sv.dtype), sv], axis=1)
        sya = diff_attention(sq, k_all, v_all, cid_q, cid_k, lam, lam_init, subln_g[l]).reshape(Bs, S, D_ATTN)
        nk_s.append(sk.reshape(Bs, S, N_HEADS, 2 * D_QK))
        nv_s.append(sv)
        nc_s.append(conv_s)
        h_p = layer_tail(h_p, yc, ya, gc, ga, w_branch[l], w_out[l], ln1_g[l], ln1_b[l], w_group[l], b_group[l], w_router[l], b_router[l], w_gate_up[l], w_down[l], ln2_g[l], ln2_b[l])
        h_s = layer_tail(h_s, syc, sya, sgc, sga, w_branch[l], w_out[l], ln1_g[l], ln1_b[l], w_group[l], b_group[l], w_router[l], b_router[l], w_gate_up[l], w_down[l], ln2_g[l], ln2_b[l])
    y_prompt = h_p[:, N_META:]
    y_sample = h_s
    return (y_prompt, y_sample, jnp.stack(nk_p), jnp.stack(nv_p), jnp.stack(nc_p), jnp.stack(nk_s), jnp.stack(nv_s), jnp.stack(nc_s))
```

```python
import functools
import math

import jax
import jax.numpy as jnp
from jax import lax
from jax.experimental import pallas as pl
from jax.experimental.pallas import tpu as pltpu

CHUNK = 64
CHUNK_SHIFT = CHUNK.bit_length() - 1
assert 1 << CHUNK_SHIFT == CHUNK
LANES = 128
LN_EPS = 1e-5
RMS_EPS = 1e-5
HIST = 8
ROUTE_LANES = 128
VMEM_LIMIT = 52 * 1024 * 1024

F32 = jnp.float32
BF16 = jnp.bfloat16


def _const_spec(shape):
    return pl.BlockSpec(shape, lambda *_: (0,) * len(shape), pipeline_mode=pl.Buffered(1))


def _layer_norm(x, g, b):
    mu = jnp.mean(x, axis=-1, keepdims=True)
    xc = x - mu
    var = jnp.mean(xc * xc, axis=-1, keepdims=True)
    return xc * lax.rsqrt(var + LN_EPS) * g + b


def _proj_conv_kernel(x_ref, c0_ref, win_ref, cw_ref, wb0_ref,
                      k32_ref, v32_ref, q_ref, kb_ref, vb_ref, mc_ref, sga_ref, nc_ref,
                      carry_ref, *, bb, ts, d, q_scale):
    @pl.when(pl.program_id(1) == 0)
    def _():
        carry_ref[...] = c0_ref[...]

    xb = x_ref[...].reshape(bb * ts, d).astype(BF16)

    def proj(i):
        return jnp.dot(xb, win_ref[:, i * d:(i + 1) * d], preferred_element_type=F32)

    u = proj(2) * proj(0)
    cw = cw_ref[...]
    row = lax.broadcasted_iota(jnp.int32, (ts, d), 0)
    convs = []
    for b in range(bb):
        ub = u[b * ts:(b + 1) * ts]
        hist = carry_ref[b]
        h1 = hist[HIST - 1:HIST]
        h2 = hist[HIST - 2:HIST - 1]
        um1 = jnp.where(row == 0, h1, pltpu.roll(ub, 1, 0))
        um2 = jnp.where(row == 0, h2, jnp.where(row == 1, h1, pltpu.roll(ub, 2, 0)))
        convs.append(cw[0:1] * um2 + cw[1:2] * um1 + cw[2:3] * ub)
        carry_ref[b] = ub[ts - HIST:ts]
        nc_ref[b] = ub[ts - HIST:ts]
    conv = convs[0] if bb == 1 else jnp.concatenate(convs, axis=0)
    yc = (proj(1) * conv).astype(BF16)
    mc = jax.nn.sigmoid(proj(6)) * jnp.dot(yc, wb0_ref[...], preferred_element_type=F32)
    mc_ref[...] = mc.astype(BF16).reshape(bb, ts, d)
    sga_ref[...] = jax.nn.sigmoid(proj(7)).astype(BF16).reshape(bb, ts, d)
    q_ref[...] = (proj(3) * q_scale).astype(BF16).reshape(bb, ts, d)
    k = proj(4)
    k32_ref[...] = k.reshape(bb, ts, d)
    kb_ref[...] = k.astype(BF16).reshape(bb, ts, d)
    v = proj(5)
    v32_ref[...] = v.reshape(bb, ts, d)
    vb_ref[...] = v.astype(BF16).reshape(bb, ts, d)


def _proj_conv(x, c0, w_in, conv_w, wb0, *, bb, ts, q_scale):
    nb, length, d = x.shape
    assert nb % bb == 0 and length % ts == 0 and ts % HIST == 0
    assert w_in.shape == (d, 8 * d), "all eight projection sections must be d_model wide"
    blk = pl.BlockSpec((bb, ts, d), lambda b, s: (b, s, 0))
    hist_blk = pl.BlockSpec((bb, HIST, d), lambda b, s: (b, 0, 0))
    f32_out = jax.ShapeDtypeStruct((nb, length, d), F32)
    bf_out = jax.ShapeDtypeStruct((nb, length, d), BF16)
    return pl.pallas_call(
        functools.partial(_proj_conv_kernel, bb=bb, ts=ts, d=d, q_scale=q_scale),
        grid=(nb // bb, length // ts),
        in_specs=[blk, hist_blk, _const_spec((d, 8 * d)), _const_spec((3, d)), _const_spec((d, d))],
        out_specs=[blk, blk, blk, blk, blk, blk, blk, hist_blk],
        out_shape=[f32_out, f32_out, bf_out, bf_out, bf_out, bf_out, bf_out,
                   jax.ShapeDtypeStruct((nb, HIST, d), F32)],
        scratch_shapes=[pltpu.VMEM((bb, HIST, d), F32)],
        compiler_params=pltpu.CompilerParams(
            dimension_semantics=("parallel", "arbitrary"), vmem_limit_bytes=VMEM_LIMIT),
        name="proj_conv",
    )(x, c0, w_in, conv_w, wb0)


def _pad_keys(x):
    pad = -x.shape[0] % LANES
    return x if pad == 0 else jnp.concatenate([x, jnp.zeros((pad, x.shape[1]), x.dtype)], axis=0)


def _attn_kernel(lqk_ref, g_ref, q_ref, ck_ref, cv_ref, kf_ref, vf_ref, o_ref,
                 *, tq, frame0, lam_init, dqk):
    qi = pl.program_id(2)
    lqk = lqk_ref[...]
    lam = (jnp.exp(jnp.sum(lqk[0:1] * lqk[1:2], axis=-1, keepdims=True))
           - jnp.exp(jnp.sum(lqk[2:3] * lqk[3:4], axis=-1, keepdims=True)) + lam_init)

    q = q_ref[0]
    lane = lax.broadcasted_iota(jnp.int32, q.shape, 1)
    zero = jnp.zeros_like(q)
    qs = jnp.concatenate([jnp.where(lane < dqk, q, zero), jnp.where(lane >= dqk, q, zero)], axis=0)

    def scores(k):
        return lax.dot_general(qs, k, (((1,), (1,)), ((), ())), preferred_element_type=F32)

    n_ctx = ck_ref.shape[1]
    s = scores(_pad_keys(ck_ref[0].astype(BF16)))
    if n_ctx % LANES:
        s = jnp.where(lax.broadcasted_iota(jnp.int32, s.shape, 1) < n_ctx, s, -jnp.inf)
    m = jnp.max(s, axis=-1, keepdims=True)
    p = jnp.exp(s - m)
    l = jnp.sum(p, axis=-1, keepdims=True)
    acc = jnp.dot(p.astype(BF16), _pad_keys(cv_ref[0].astype(BF16)), preferred_element_type=F32)

    def step(j, carry, masked):
        m, l, acc = carry
        start = pl.multiple_of(j * tq, tq)
        s = scores(_pad_keys(kf_ref[0, pl.ds(start, tq), :]))
        if masked:
            r = lax.broadcasted_iota(jnp.int32, s.shape, 0)
            c = lax.broadcasted_iota(jnp.int32, s.shape, 1)
            q_chunk = lax.shift_right_logical(frame0 + j * tq + jnp.where(r >= tq, r - tq, r), CHUNK_SHIFT)
            k_chunk = lax.shift_right_logical(frame0 + j * tq + c, CHUNK_SHIFT)
            s = jnp.where((k_chunk <= q_chunk) & (c < tq), s, -jnp.inf)
        m_new = jnp.maximum(m, jnp.max(s, axis=-1, keepdims=True))
        alpha = jnp.exp(m - m_new)
        p = jnp.exp(s - m_new)
        l = alpha * l + jnp.sum(p, axis=-1, keepdims=True)
        acc = alpha * acc + jnp.dot(p.astype(BF16), _pad_keys(vf_ref[0, pl.ds(start, tq), :]),
                                    preferred_element_type=F32)
        return m_new, l, acc

    carry = lax.fori_loop(0, qi, functools.partial(step, masked=False), (m, l, acc))
    m, l, acc = step(qi, carry, masked=True)

    o = acc[:tq] / l[:tq] - lam * (acc[tq:] / l[tq:])
    o = o * lax.rsqrt(jnp.mean(o * o, axis=-1, keepdims=True) + RMS_EPS) * g_ref[...] * (1.0 - lam_init)
    o_ref[0] = o.astype(o_ref.dtype)


def _attention(lqk, subln_g, q, ctx_k, ctx_v, kf, vf, *, n_heads, tq, frame0, lam_init):
    nb, length, d = q.shape
    dv = d // n_heads
    dqk = lqk.shape[1]
    assert dv == 2 * dqk and length % tq == 0
    assert tq % LANES == 0 or length == tq, "only a single (diagonal) frame tile may be lane-padded"
    n_ctx = ctx_k.shape[1]
    ctx_map = (lambda b, h, i: (b, 0, h)) if ctx_k.shape[0] == nb else (lambda b, h, i: (0, 0, h))
    q_blk = pl.BlockSpec((1, tq, dv), lambda b, h, i: (b, i, h))
    ctx_blk = pl.BlockSpec((1, n_ctx, dv), ctx_map)
    kv_blk = pl.BlockSpec((1, length, dv), lambda b, h, i: (b, 0, h))
    return pl.pallas_call(
        functools.partial(_attn_kernel, tq=tq, frame0=frame0, lam_init=lam_init, dqk=dqk),
        grid=(nb, n_heads, length // tq),
        in_specs=[pl.BlockSpec((4, dqk), lambda b, h, i: (0, 0)),
                  pl.BlockSpec((1, dv), lambda b, h, i: (0, 0)),
                  q_blk, ctx_blk, ctx_blk, kv_blk, kv_blk],
        out_specs=q_blk,
        out_shape=jax.ShapeDtypeStruct((nb, length, d), BF16),
        compiler_params=pltpu.CompilerParams(
            dimension_semantics=("parallel", "parallel", "arbitrary"), vmem_limit_bytes=VMEM_LIMIT),
        name="diff_attention",
    )(lqk, subln_g, q, ctx_k, ctx_v, kf, vf)


def _tail_kernel(x_ref, mc_ref, sga_ref, ya_ref, wb1_ref, wout_ref, g_ref, b_ref, wrt_ref, brt_ref,
                 h32_ref, hb_ref, gates_ref, *, alpha, n_experts, n_groups):
    merged = mc_ref[...].astype(F32) + sga_ref[...].astype(F32) * jnp.dot(
        ya_ref[...], wb1_ref[...], preferred_element_type=F32)
    y = alpha * x_ref[...] + jnp.dot(merged.astype(BF16), wout_ref[...], preferred_element_type=F32)
    h = _layer_norm(y, g_ref[...], b_ref[...])
    h32_ref[...] = h
    hb_ref[...] = h.astype(BF16)

    logits = jnp.dot(h, wrt_ref[...], preferred_element_type=F32,
                     precision=lax.Precision.HIGHEST) + brt_ref[...]
    lane = lax.broadcasted_iota(jnp.int32, logits.shape, 1)
    big = jnp.int32(ROUTE_LANES)
    neg = -jnp.inf
    per_group = n_experts // n_groups

    def first_max(v):
        vmax = jnp.max(v, axis=-1, keepdims=True)
        return vmax, jnp.min(jnp.where(v == vmax, lane, big), axis=-1, keepdims=True)

    is_group = (lane >= n_experts) & (lane < n_experts + n_groups)
    lg = jnp.where(is_group, logits, neg)
    eg = jnp.exp(lg - jnp.max(lg, axis=-1, keepdims=True))
    p_group = jnp.where(is_group, eg / jnp.sum(eg, axis=-1, keepdims=True), neg)
    p_g, g_lane = first_max(p_group)
    g_idx = g_lane - n_experts
    lo = g_idx * per_group
    le = jnp.where((lane >= lo) & (lane < lo + per_group), logits, neg)
    v1, i1 = first_max(le)
    v2, i2 = first_max(jnp.where(lane == i1, neg, le))
    e2 = jnp.exp(v2 - v1)
    w1 = p_g / (1.0 + e2)
    w2 = p_g * e2 / (1.0 + e2)
    gates_ref[...] = jnp.where(lane == i1, w1, jnp.where(lane == i2, w2, 0.0))


def _tail(x, mc, sga, ya, wb1, w_out, ln_g, ln_b, w_rt, b_rt, *, tm, alpha, n_experts, n_groups):
    t, d = x.shape
    assert t % tm == 0
    row = lambda w: pl.BlockSpec((tm, w), lambda i: (i, 0))
    return pl.pallas_call(
        functools.partial(_tail_kernel, alpha=alpha, n_experts=n_experts, n_groups=n_groups),
        grid=(t // tm,),
        in_specs=[row(d), row(d), row(d), row(d), _const_spec((d, d)), _const_spec((d, d)),
                  _const_spec((1, d)), _const_spec((1, d)),
                  _const_spec((d, ROUTE_LANES)), _const_spec((1, ROUTE_LANES))],
        out_specs=[row(d), row(d), row(ROUTE_LANES)],
        out_shape=[jax.ShapeDtypeStruct((t, d), F32), jax.ShapeDtypeStruct((t, d), BF16),
                   jax.ShapeDtypeStruct((t, ROUTE_LANES), F32)],
        compiler_params=pltpu.CompilerParams(
            dimension_semantics=("parallel",), vmem_limit_bytes=VMEM_LIMIT),
        name="merge_ln_router",
    )(x, mc, sga, ya, wb1, w_out, ln_g, ln_b, w_rt, b_rt)


def _moe_kernel(h32_ref, hb_ref, gates_ref, wgu_ref, wd_ref, g_ref, b_ref, o_ref, acc_ref,
                *, alpha, d_expert):
    e = pl.program_id(1)

    @pl.when(e == 0)
    def _():
        acc_ref[...] = jnp.zeros_like(acc_ref)

    ab = jnp.dot(hb_ref[...], wgu_ref[0], preferred_element_type=F32)
    a = ab[:, :d_expert]
    act = (a * jax.nn.sigmoid(a) * ab[:, d_expert:]).astype(BF16)
    gates = gates_ref[...]
    lane = lax.broadcasted_iota(jnp.int32, gates.shape, 1)
    gate = jnp.sum(jnp.where(lane == e, gates, 0.0), axis=-1, keepdims=True)
    acc_ref[...] += gate * jnp.dot(act, wd_ref[0], preferred_element_type=F32)

    @pl.when(e == pl.num_programs(1) - 1)
    def _():
        o_ref[...] = _layer_norm(alpha * h32_ref[...] + acc_ref[...], g_ref[...], b_ref[...])


def _moe(h32, hb, gates, wgu, wd, ln_g, ln_b, *, tm, alpha):
    t, d = h32.shape
    n_experts, _, two_de = wgu.shape
    assert t % tm == 0
    row = lambda w: pl.BlockSpec((tm, w), lambda i, e: (i, 0))
    return pl.pallas_call(
        functools.partial(_moe_kernel, alpha=alpha, d_expert=two_de // 2),
        grid=(t // tm, n_experts),
        in_specs=[row(d), row(d), row(ROUTE_LANES),
                  pl.BlockSpec((1, d, two_de), lambda i, e: (e, 0, 0)),
                  pl.BlockSpec((1, two_de // 2, d), lambda i, e: (e, 0, 0)),
                  pl.BlockSpec((1, d), lambda i, e: (0, 0)), pl.BlockSpec((1, d), lambda i, e: (0, 0))],
        out_specs=row(d),
        out_shape=jax.ShapeDtypeStruct((t, d), F32),
        scratch_shapes=[pltpu.VMEM((tm, d), F32)],
        compiler_params=pltpu.CompilerParams(
            dimension_semantics=("parallel", "arbitrary"), vmem_limit_bytes=VMEM_LIMIT),
        name="moe_ln",
    )(h32, hb, gates, wgu, wd, ln_g, ln_b)


def _pick_tile(n, target):
    t = min(n, target)
    while n % t:
        t //= 2
    return t


def kernel(x_prompt, x_sample, cache_k, cache_v, state_conv, meta_tokens, w_in, conv_w, lambda_qk, subln_g, w_branch, w_out, ln1_g, ln1_b, w_group, b_group, w_router, b_router, w_gate_up, w_down, ln2_g, ln2_b):
    depth = w_in.shape[0]
    assert depth == 1, "single-layer step only"
    bp, seq, d = x_prompt.shape
    bs, s_len, _ = x_sample.shape
    n_meta = meta_tokens.shape[0]
    n_heads = cache_k.shape[3]
    dqk = cache_k.shape[4] // 2
    past = cache_k.shape[2] - n_meta
    n_groups = w_group.shape[-1]
    n_experts = w_router.shape[-1]
    assert n_experts + n_groups <= ROUTE_LANES
    q_scale = dqk ** -0.5
    assert math.frexp(q_scale)[0] == 0.5, "the score scale is folded into q; exact only for powers of two"
    alpha = (2.0 * depth) ** 0.25
    lam_init = 0.8 - 0.6 * math.exp(-0.3 * 0)

    w_in_b = w_in[0].astype(BF16)
    wb = w_branch[0].astype(BF16)
    w_out_b = w_out[0].astype(BF16)
    wgu_b = w_gate_up[0].astype(BF16)
    wd_b = w_down[0].astype(BF16)
    w_rt = jnp.zeros((d, ROUTE_LANES), F32).at[:, :n_experts].set(w_router[0]).at[
        :, n_experts:n_experts + n_groups].set(w_group[0])
    b_rt = jnp.zeros((1, ROUTE_LANES), F32).at[0, :n_experts].set(b_router[0]).at[
        0, n_experts:n_experts + n_groups].set(b_group[0])

    def hist_rows(rows):
        return jnp.pad(rows, ((0, 0), (HIST - rows.shape[1], 0), (0, 0)))

    proj = functools.partial(_proj_conv, w_in=w_in_b, conv_w=conv_w[0], wb0=wb[0], q_scale=q_scale)
    attn = functools.partial(_attention, lambda_qk[0], subln_g, n_heads=n_heads, lam_init=lam_init)
    tail = functools.partial(_tail, wb1=wb[1], w_out=w_out_b, ln_g=ln1_g, ln_b=ln1_b, w_rt=w_rt, b_rt=b_rt,
                             alpha=alpha, n_experts=n_experts, n_groups=n_groups)
    moe = functools.partial(_moe, wgu=wgu_b, wd=wd_b, ln_g=ln2_g, ln_b=ln2_b, alpha=alpha)

    mk32, mv32, _, mkb, mvb, _, _, mnc = proj(
        meta_tokens[None], jnp.zeros((1, HIST, d), F32), bb=1, ts=n_meta)

    ts = _pick_tile(seq, 256)
    k32, v32, qb, kb, vb, mc, sga, nc = proj(
        x_prompt, jnp.broadcast_to(mnc, (bp, HIST, d)), bb=1, ts=ts)
    ya = attn(qb, mkb, mvb, kb, vb, tq=_pick_tile(seq, 256), frame0=0)
    t_p = bp * seq
    tm = _pick_tile(t_p, 512)
    h32, hb, gates = tail(x_prompt.reshape(t_p, d), mc.reshape(t_p, d), sga.reshape(t_p, d),
                          ya.reshape(t_p, d), tm=tm)
    y_prompt = moe(h32, hb, gates, tm=_pick_tile(t_p, 1024)).reshape(bp, seq, d)

    sk32, sv32, sqb, skb, svb, smc, ssga, snc = proj(
        x_sample, hist_rows(state_conv[0]), bb=bs, ts=s_len)
    sya = attn(sqb, cache_k[0].reshape(bs, n_meta + past, d), cache_v[0].reshape(bs, n_meta + past, d),
               skb, svb, tq=s_len, frame0=past)
    t_s = bs * s_len
    sh32, shb, sgates = tail(x_sample.reshape(t_s, d), smc.reshape(t_s, d), ssga.reshape(t_s, d),
                             sya.reshape(t_s, d), tm=t_s)
    y_sample = moe(sh32, shb, sgates, tm=t_s).reshape(bs, s_len, d)

    def with_meta(m, f):
        full = jnp.concatenate([jnp.broadcast_to(m, (bp, n_meta, d)), f], axis=1)
        return full.reshape(1, bp, n_meta + seq, n_heads, d // n_heads)

    return (y_prompt, y_sample,
            with_meta(mk32, k32), with_meta(mv32, v32), nc[None, :, HIST - 2:],
            sk32.reshape(1, bs, s_len, n_heads, d // n_heads),
            sv32.reshape(1, bs, s_len, n_heads, d // n_heads), snc[None, :, HIST - 2:])
```

```python
import functools
import math

import jax
import jax.numpy as jnp
from jax import lax
from jax.experimental import pallas as pl
from jax.experimental.pallas import tpu as pltpu

CHUNK = 64
CHUNK_SHIFT = CHUNK.bit_length() - 1
assert 1 << CHUNK_SHIFT == CHUNK
LANES = 128
BIAS_META, BIAS_DIAG = 0, 1
LN_EPS = 1e-5
RMS_EPS = 1e-5
HIST = 8
ROUTE_LANES = 128
VMEM_LIMIT = 52 * 1024 * 1024

F32 = jnp.float32
BF16 = jnp.bfloat16


def _const_spec(shape):
    return pl.BlockSpec(shape, lambda *_: (0,) * len(shape), pipeline_mode=pl.Buffered(1))


def _layer_norm(x, g, b):
    mu = jnp.mean(x, axis=-1, keepdims=True)
    xc = x - mu
    var = jnp.mean(xc * xc, axis=-1, keepdims=True)
    return xc * lax.rsqrt(var + LN_EPS) * g + b


def _proj_conv_kernel(x_ref, c0_ref, win_ref, cw_ref, wb0_ref, wqvt_ref,
                      k32_ref, v32_ref, q_ref, kb_ref, vb_ref, mc_ref, sga_ref, nc_ref,
                      carry_ref, *, bb, ts, d, q_scale, transposed):
    @pl.when(pl.program_id(1) == 0)
    def _():
        carry_ref[...] = c0_ref[...]

    xb = x_ref[...].reshape(bb * ts, d).astype(BF16)

    def proj(i):
        return jnp.dot(xb, win_ref[:, i * d:(i + 1) * d], preferred_element_type=F32)

    u = proj(2) * proj(0)
    cw = cw_ref[...]
    row = lax.broadcasted_iota(jnp.int32, (ts, d), 0)
    convs = []
    for b in range(bb):
        ub = u[b * ts:(b + 1) * ts]
        hist = carry_ref[b]
        h1 = hist[HIST - 1:HIST]
        h2 = hist[HIST - 2:HIST - 1]
        um1 = jnp.where(row == 0, h1, pltpu.roll(ub, 1, 0))
        um2 = jnp.where(row == 0, h2, jnp.where(row == 1, h1, pltpu.roll(ub, 2, 0)))
        convs.append(cw[0:1] * um2 + cw[1:2] * um1 + cw[2:3] * ub)
        carry_ref[b] = ub[ts - HIST:ts]
        nc_ref[b] = ub[ts - HIST:ts]
    conv = convs[0] if bb == 1 else jnp.concatenate(convs, axis=0)
    yc = (proj(1) * conv).astype(BF16)
    mc = jax.nn.sigmoid(proj(6)) * jnp.dot(yc, wb0_ref[...], preferred_element_type=F32)
    mc_ref[...] = mc.astype(BF16).reshape(bb, ts, d)
    sga_ref[...] = jax.nn.sigmoid(proj(7)).astype(BF16).reshape(bb, ts, d)
    k = proj(4)
    k32_ref[...] = k.reshape(bb, ts, d)
    kb_ref[...] = k.astype(BF16).reshape(bb, ts, d)
    v = proj(5)
    v32_ref[...] = v.reshape(bb, ts, d)
    if transposed:
        nt = (((1,), (1,)), ((), ()))
        qt = lax.dot_general(wqvt_ref[0:d, :], xb, nt, preferred_element_type=F32)
        q_ref[0, 0] = (qt * q_scale).astype(BF16)
        vb_ref[0, 0] = lax.dot_general(wqvt_ref[d:2 * d, :], xb, nt, preferred_element_type=F32).astype(BF16)
    else:
        q_ref[...] = (proj(3) * q_scale).astype(BF16).reshape(bb, ts, d)
        vb_ref[...] = v.astype(BF16).reshape(bb, ts, d)


def _proj_conv(x, c0, w_in, conv_w, wb0, wqvt, *, bb, ts, q_scale, transposed):
    nb, length, d = x.shape
    assert nb % bb == 0 and length % ts == 0 and ts % HIST == 0
    assert w_in.shape == (d, 8 * d), "all eight projection sections must be d_model wide"
    assert not transposed or bb == 1
    blk = pl.BlockSpec((bb, ts, d), lambda b, s: (b, s, 0))
    hist_blk = pl.BlockSpec((bb, HIST, d), lambda b, s: (b, 0, 0))
    f32_out = jax.ShapeDtypeStruct((nb, length, d), F32)
    bf_out = jax.ShapeDtypeStruct((nb, length, d), BF16)
    if transposed:
        qv_blk = pl.BlockSpec((1, 1, d, ts), lambda b, s: (b, s, 0, 0))
        qv_out = jax.ShapeDtypeStruct((nb, length // ts, d, ts), BF16)
    else:
        qv_blk, qv_out = blk, bf_out
    return pl.pallas_call(
        functools.partial(_proj_conv_kernel, bb=bb, ts=ts, d=d, q_scale=q_scale, transposed=transposed),
        grid=(nb // bb, length // ts),
        in_specs=[blk, hist_blk, _const_spec((d, 8 * d)), _const_spec((3, d)), _const_spec((d, d)),
                  _const_spec((2 * d, d))],
        out_specs=[blk, blk, qv_blk, blk, qv_blk, blk, blk, hist_blk],
        out_shape=[f32_out, f32_out, qv_out, bf_out, qv_out, bf_out, bf_out,
                   jax.ShapeDtypeStruct((nb, HIST, d), F32)],
        scratch_shapes=[pltpu.VMEM((bb, HIST, d), F32)],
        compiler_params=pltpu.CompilerParams(
            dimension_semantics=("parallel", "arbitrary"), vmem_limit_bytes=VMEM_LIMIT),
        name="proj_conv",
    )(x, c0, w_in, conv_w, wb0, wqvt)


def _pad_keys(x):
    pad = -x.shape[0] % LANES
    return x if pad == 0 else jnp.concatenate([x, jnp.zeros((pad, x.shape[1]), x.dtype)], axis=0)


def _attn_kernel(lqk_ref, g_ref, q_ref, ck_ref, cv_ref, kf_ref, vf_ref, o_ref,
                 *, tq, frame0, lam_init, dqk):
    qi = pl.program_id(2)
    lqk = lqk_ref[...]
    lam = (jnp.exp(jnp.sum(lqk[0:1] * lqk[1:2], axis=-1, keepdims=True))
           - jnp.exp(jnp.sum(lqk[2:3] * lqk[3:4], axis=-1, keepdims=True)) + lam_init)

    q = q_ref[0]
    lane = lax.broadcasted_iota(jnp.int32, q.shape, 1)
    zero = jnp.zeros_like(q)
    qs = jnp.concatenate([jnp.where(lane < dqk, q, zero), jnp.where(lane >= dqk, q, zero)], axis=0)

    def scores(k):
        return lax.dot_general(qs, k, (((1,), (1,)), ((), ())), preferred_element_type=F32)

    n_ctx = ck_ref.shape[1]
    s = scores(_pad_keys(ck_ref[0].astype(BF16)))
    if n_ctx % LANES:
        s = jnp.where(lax.broadcasted_iota(jnp.int32, s.shape, 1) < n_ctx, s, -jnp.inf)
    m = jnp.max(s, axis=-1, keepdims=True)
    p = jnp.exp(s - m)
    l = jnp.sum(p, axis=-1, keepdims=True)
    acc = jnp.dot(p.astype(BF16), _pad_keys(cv_ref[0].astype(BF16)), preferred_element_type=F32)

    def step(j, carry, masked):
        m, l, acc = carry
        start = pl.multiple_of(j * tq, tq)
        s = scores(_pad_keys(kf_ref[0, pl.ds(start, tq), :]))
        if masked:
            r = lax.broadcasted_iota(jnp.int32, s.shape, 0)
            c = lax.broadcasted_iota(jnp.int32, s.shape, 1)
            q_chunk = lax.shift_right_logical(frame0 + j * tq + jnp.where(r >= tq, r - tq, r), CHUNK_SHIFT)
            k_chunk = lax.shift_right_logical(frame0 + j * tq + c, CHUNK_SHIFT)
            s = jnp.where((k_chunk <= q_chunk) & (c < tq), s, -jnp.inf)
        m_new = jnp.maximum(m, jnp.max(s, axis=-1, keepdims=True))
        alpha = jnp.exp(m - m_new)
        p = jnp.exp(s - m_new)
        l = alpha * l + jnp.sum(p, axis=-1, keepdims=True)
        acc = alpha * acc + jnp.dot(p.astype(BF16), _pad_keys(vf_ref[0, pl.ds(start, tq), :]),
                                    preferred_element_type=F32)
        return m_new, l, acc

    carry = lax.fori_loop(0, qi, functools.partial(step, masked=False), (m, l, acc))
    m, l, acc = step(qi, carry, masked=True)

    o = acc[:tq] / l[:tq] - lam * (acc[tq:] / l[tq:])
    o = o * lax.rsqrt(jnp.mean(o * o, axis=-1, keepdims=True) + RMS_EPS) * g_ref[...] * (1.0 - lam_init)
    o_ref[0] = o.astype(o_ref.dtype)


def _attention(lqk, subln_g, q, ctx_k, ctx_v, kf, vf, *, n_heads, tq, frame0, lam_init):
    nb, length, d = q.shape
    dv = d // n_heads
    dqk = lqk.shape[1]
    assert dv == 2 * dqk and length % tq == 0
    assert tq % LANES == 0 or length == tq, "only a single (diagonal) frame tile may be lane-padded"
    n_ctx = ctx_k.shape[1]
    ctx_map = (lambda b, h, i: (b, 0, h)) if ctx_k.shape[0] == nb else (lambda b, h, i: (0, 0, h))
    q_blk = pl.BlockSpec((1, tq, dv), lambda b, h, i: (b, i, h))
    ctx_blk = pl.BlockSpec((1, n_ctx, dv), ctx_map)
    kv_blk = pl.BlockSpec((1, length, dv), lambda b, h, i: (b, 0, h))
    return pl.pallas_call(
        functools.partial(_attn_kernel, tq=tq, frame0=frame0, lam_init=lam_init, dqk=dqk),
        grid=(nb, n_heads, length // tq),
        in_specs=[pl.BlockSpec((4, dqk), lambda b, h, i: (0, 0)),
                  pl.BlockSpec((1, dv), lambda b, h, i: (0, 0)),
                  q_blk, ctx_blk, ctx_blk, kv_blk, kv_blk],
        out_specs=q_blk,
        out_shape=jax.ShapeDtypeStruct((nb, length, d), BF16),
        compiler_params=pltpu.CompilerParams(
            dimension_semantics=("parallel", "parallel", "arbitrary"), vmem_limit_bytes=VMEM_LIMIT),
        name="diff_attention",
    )(lqk, subln_g, q, ctx_k, ctx_v, kf, vf)


def _attn_t_kernel(lqk_ref, g_ref, bias_ref, qt_ref, mk_ref, mvt_ref, kf_ref, vt_ref, o_ref,
                   *, tq, n_tiles, lam_init, dqk):
    lqk = lqk_ref[...]
    lam = (jnp.exp(jnp.sum(lqk[0:1] * lqk[1:2], axis=-1, keepdims=True))
           - jnp.exp(jnp.sum(lqk[2:3] * lqk[3:4], axis=-1, keepdims=True)) + lam_init)

    qt = qt_ref[0, 0]
    row = lax.broadcasted_iota(jnp.int32, qt.shape, 0)
    zero = jnp.zeros_like(qt)
    w = jnp.concatenate([jnp.where(row < dqk, qt, zero), jnp.where(row >= dqk, qt, zero)], axis=1)

    def query_tile(q):
        n = q + 2

        def scores(t):
            if t == 0:
                return jnp.dot(mk_ref[...], w, preferred_element_type=F32) + bias_ref[BIAS_META]
            s = jnp.dot(kf_ref[0, (t - 1) * tq:t * tq, :], w, preferred_element_type=F32)
            return s + bias_ref[BIAS_DIAG] if t == n - 1 else s

        s = scores(0)
        for t in range(n):
            s_next = scores(t + 1) if t + 1 < n else None
            vt = mvt_ref[...] if t == 0 else vt_ref[0, t - 1]
            s_max = jnp.max(s, axis=0, keepdims=True)
            if t == 0:
                m = s_max
                p = jnp.exp(s - m)
                l = jnp.sum(p, axis=0, keepdims=True)
                acc = jnp.dot(vt, p.astype(BF16), preferred_element_type=F32)
            else:
                m_new = jnp.maximum(m, s_max)
                alpha = jnp.exp(m - m_new)
                p = jnp.exp(s - m_new)
                l = alpha * l + jnp.sum(p, axis=0, keepdims=True)
                acc = alpha * acc + jnp.dot(vt, p.astype(BF16), preferred_element_type=F32)
                m = m_new
            s = s_next

        ot = acc[:, :tq] / l[:, :tq] - lam * (acc[:, tq:] / l[:, tq:])
        ot = ot * lax.rsqrt(jnp.mean(ot * ot, axis=0, keepdims=True) + RMS_EPS)
        o_ref[0] = (ot.T * g_ref[...] * (1.0 - lam_init)).astype(o_ref.dtype)

    for q in range(n_tiles):
        pl.when(pl.program_id(2) == q)(functools.partial(query_tile, q))


def _attention_bias(tq, n_meta):
    r = jnp.arange(tq, dtype=jnp.int32)[:, None]
    c = jnp.arange(2 * tq, dtype=jnp.int32)[None, :] % tq
    meta = jnp.broadcast_to(r < n_meta, (tq, 2 * tq))
    diag = (r // CHUNK) <= (c // CHUNK)
    neg = jnp.float32(-jnp.inf)
    return jnp.stack([jnp.where(meta, 0.0, neg), jnp.where(diag, 0.0, neg)])


def _attention_t(lqk, subln_g, qt, mk, mvt, kf, vt, *, n_heads, n_meta, lam_init):
    nb, n_tiles, d, tq = qt.shape
    length = n_tiles * tq
    dv = d // n_heads
    dqk = lqk.shape[1]
    assert dv == 2 * dqk and kf.shape == (nb, length, d) and vt.shape == qt.shape
    assert mk.shape == (tq, d) and mvt.shape == (d, tq) and n_meta <= tq and tq % CHUNK == 0
    return pl.pallas_call(
        functools.partial(_attn_t_kernel, tq=tq, n_tiles=n_tiles, lam_init=lam_init, dqk=dqk),
        grid=(nb, n_heads, n_tiles),
        in_specs=[pl.BlockSpec((4, dqk), lambda b, h, i: (0, 0)),
                  pl.BlockSpec((1, dv), lambda b, h, i: (0, 0)),
                  _const_spec((2, tq, 2 * tq)),
                  pl.BlockSpec((1, 1, dv, tq), lambda b, h, i: (b, i, h, 0)),
                  pl.BlockSpec((tq, dv), lambda b, h, i: (0, h)),
                  pl.BlockSpec((dv, tq), lambda b, h, i: (h, 0)),
                  pl.BlockSpec((1, length, dv), lambda b, h, i: (b, 0, h)),
                  pl.BlockSpec((1, n_tiles, dv, tq), lambda b, h, i: (b, 0, h, 0))],
        out_specs=pl.BlockSpec((1, tq, dv), lambda b, h, i: (b, i, h)),
        out_shape=jax.ShapeDtypeStruct((nb, length, d), BF16),
        compiler_params=pltpu.CompilerParams(
            dimension_semantics=("parallel", "parallel", "arbitrary"), vmem_limit_bytes=VMEM_LIMIT),
        name="diff_attention_t",
    )(lqk, subln_g, _attention_bias(tq, n_meta), qt, mk, mvt, kf, vt)


def _tail_kernel(x_ref, mc_ref, sga_ref, ya_ref, wb1_ref, wout_ref, g_ref, b_ref, wrt_ref, brt_ref,
                 h32_ref, hb_ref, gates_ref, *, alpha, n_experts, n_groups):
    merged = mc_ref[...].astype(F32) + sga_ref[...].astype(F32) * jnp.dot(
        ya_ref[...], wb1_ref[...], preferred_element_type=F32)
    y = alpha * x_ref[...] + jnp.dot(merged.astype(BF16), wout_ref[...], preferred_element_type=F32)
    h = _layer_norm(y, g_ref[...], b_ref[...])
    h32_ref[...] = h
    hb_ref[...] = h.astype(BF16)

    logits = jnp.dot(h, wrt_ref[...], preferred_element_type=F32,
                     precision=lax.Precision.HIGHEST) + brt_ref[...]
    lane = lax.broadcasted_iota(jnp.int32, logits.shape, 1)
    big = jnp.int32(ROUTE_LANES)
    neg = -jnp.inf
    per_group = n_experts // n_groups

    def first_max(v):
        vmax = jnp.max(v, axis=-1, keepdims=True)
        return vmax, jnp.min(jnp.where(v == vmax, lane, big), axis=-1, keepdims=True)

    is_group = (lane >= n_experts) & (lane < n_experts + n_groups)
    lg = jnp.where(is_group, logits, neg)
    eg = jnp.exp(lg - jnp.max(lg, axis=-1, keepdims=True))
    p_group = jnp.where(is_group, eg / jnp.sum(eg, axis=-1, keepdims=True), neg)
    p_g, g_lane = first_max(p_group)
    g_idx = g_lane - n_experts
    lo = g_idx * per_group
    le = jnp.where((lane >= lo) & (lane < lo + per_group), logits, neg)
    v1, i1 = first_max(le)
    v2, i2 = first_max(jnp.where(lane == i1, neg, le))
    e2 = jnp.exp(v2 - v1)
    w1 = p_g / (1.0 + e2)
    w2 = p_g * e2 / (1.0 + e2)
    gates_ref[...] = jnp.where(lane == i1, w1, jnp.where(lane == i2, w2, 0.0))


def _tail(x, mc, sga, ya, wb1, w_out, ln_g, ln_b, w_rt, b_rt, *, tm, alpha, n_experts, n_groups):
    t, d = x.shape
    assert t % tm == 0
    row = lambda w: pl.BlockSpec((tm, w), lambda i: (i, 0))
    return pl.pallas_call(
        functools.partial(_tail_kernel, alpha=alpha, n_experts=n_experts, n_groups=n_groups),
        grid=(t // tm,),
        in_specs=[row(d), row(d), row(d), row(d), _const_spec((d, d)), _const_spec((d, d)),
                  _const_spec((1, d)), _const_spec((1, d)),
                  _const_spec((d, ROUTE_LANES)), _const_spec((1, ROUTE_LANES))],
        out_specs=[row(d), row(d), row(ROUTE_LANES)],
        out_shape=[jax.ShapeDtypeStruct((t, d), F32), jax.ShapeDtypeStruct((t, d), BF16),
                   jax.ShapeDtypeStruct((t, ROUTE_LANES), F32)],
        compiler_params=pltpu.CompilerParams(
            dimension_semantics=("parallel",), vmem_limit_bytes=VMEM_LIMIT),
        name="merge_ln_router",
    )(x, mc, sga, ya, wb1, w_out, ln_g, ln_b, w_rt, b_rt)


def _moe_kernel(h32_ref, hb_ref, gates_ref, wgu_ref, wd_ref, g_ref, b_ref, o_ref, acc_ref,
                *, alpha, d_expert):
    e = pl.program_id(1)

    @pl.when(e == 0)
    def _():
        acc_ref[...] = jnp.zeros_like(acc_ref)

    ab = jnp.dot(hb_ref[...], wgu_ref[0], preferred_element_type=F32)
    a = ab[:, :d_expert]
    act = (a * jax.nn.sigmoid(a) * ab[:, d_expert:]).astype(BF16)
    gates = gates_ref[...]
    lane = lax.broadcasted_iota(jnp.int32, gates.shape, 1)
    gate = jnp.sum(jnp.where(lane == e, gates, 0.0), axis=-1, keepdims=True)
    acc_ref[...] += gate * jnp.dot(act, wd_ref[0], preferred_element_type=F32)

    @pl.when(e == pl.num_programs(1) - 1)
    def _():
        o_ref[...] = _layer_norm(alpha * h32_ref[...] + acc_ref[...], g_ref[...], b_ref[...])


def _moe(h32, hb, gates, wgu, wd, ln_g, ln_b, *, tm, alpha):
    t, d = h32.shape
    n_experts, _, two_de = wgu.shape
    assert t % tm == 0
    row = lambda w: pl.BlockSpec((tm, w), lambda i, e: (i, 0))
    return pl.pallas_call(
        functools.partial(_moe_kernel, alpha=alpha, d_expert=two_de // 2),
        grid=(t // tm, n_experts),
        in_specs=[row(d), row(d), row(ROUTE_LANES),
                  pl.BlockSpec((1, d, two_de), lambda i, e: (e, 0, 0)),
                  pl.BlockSpec((1, two_de // 2, d), lambda i, e: (e, 0, 0)),
                  pl.BlockSpec((1, d), lambda i, e: (0, 0)), pl.BlockSpec((1, d), lambda i, e: (0, 0))],
        out_specs=row(d),
        out_shape=jax.ShapeDtypeStruct((t, d), F32),
        scratch_shapes=[pltpu.VMEM((tm, d), F32)],
        compiler_params=pltpu.CompilerParams(
            dimension_semantics=("parallel", "arbitrary"), vmem_limit_bytes=VMEM_LIMIT),
        name="moe_ln",
    )(h32, hb, gates, wgu, wd, ln_g, ln_b)


def _pick_tile(n, target):
    t = min(n, target)
    while n % t:
        t //= 2
    return t


def kernel(x_prompt, x_sample, cache_k, cache_v, state_conv, meta_tokens, w_in, conv_w, lambda_qk, subln_g, w_branch, w_out, ln1_g, ln1_b, w_group, b_group, w_router, b_router, w_gate_up, w_down, ln2_g, ln2_b):
    depth = w_in.shape[0]
    assert depth == 1, "single-layer step only"
    bp, seq, d = x_prompt.shape
    bs, s_len, _ = x_sample.shape
    n_meta = meta_tokens.shape[0]
    n_heads = cache_k.shape[3]
    dqk = cache_k.shape[4] // 2
    past = cache_k.shape[2] - n_meta
    n_groups = w_group.shape[-1]
    n_experts = w_router.shape[-1]
    assert n_experts + n_groups <= ROUTE_LANES
    q_scale = dqk ** -0.5
    assert math.frexp(q_scale)[0] == 0.5, "the score scale is folded into q; exact only for powers of two"
    alpha = (2.0 * depth) ** 0.25
    lam_init = 0.8 - 0.6 * math.exp(-0.3 * 0)

    w_in_b = w_in[0].astype(BF16)
    wb = w_branch[0].astype(BF16)
    w_out_b = w_out[0].astype(BF16)
    wgu_b = w_gate_up[0].astype(BF16)
    wd_b = w_down[0].astype(BF16)
    w_rt = jnp.zeros((d, ROUTE_LANES), F32).at[:, :n_experts].set(w_router[0]).at[
        :, n_experts:n_experts + n_groups].set(w_group[0])
    b_rt = jnp.zeros((1, ROUTE_LANES), F32).at[0, :n_experts].set(b_router[0]).at[
        0, n_experts:n_experts + n_groups].set(b_group[0])

    def hist_rows(rows):
        return jnp.pad(rows, ((0, 0), (HIST - rows.shape[1], 0), (0, 0)))

    wqvt = jnp.concatenate([w_in_b[:, 3 * d:4 * d].T, w_in_b[:, 5 * d:6 * d].T], axis=0)
    proj = functools.partial(_proj_conv, w_in=w_in_b, conv_w=conv_w[0], wb0=wb[0], wqvt=wqvt, q_scale=q_scale)
    attn = functools.partial(_attention, lambda_qk[0], subln_g, n_heads=n_heads, lam_init=lam_init)
    tail = functools.partial(_tail, wb1=wb[1], w_out=w_out_b, ln_g=ln1_g, ln_b=ln1_b, w_rt=w_rt, b_rt=b_rt,
                             alpha=alpha, n_experts=n_experts, n_groups=n_groups)
    moe = functools.partial(_moe, wgu=wgu_b, wd=wd_b, ln_g=ln2_g, ln_b=ln2_b, alpha=alpha)

    mk32, mv32, _, mkb, mvb, _, _, mnc = proj(
        meta_tokens[None], jnp.zeros((1, HIST, d), F32), bb=1, ts=n_meta, transposed=False)
    ts = _pick_tile(seq, 256)
    mk_pad = jnp.pad(mkb[0], ((0, ts - n_meta), (0, 0)))
    mvt_pad = jnp.pad(mvb[0].T, ((0, 0), (0, ts - n_meta)))
    k32, v32, qt, kb, vt, mc, sga, nc = proj(
        x_prompt, jnp.broadcast_to(mnc, (bp, HIST, d)), bb=1, ts=ts, transposed=True)
    ya = _attention_t(lambda_qk[0], subln_g, qt, mk_pad, mvt_pad, kb, vt,
                      n_heads=n_heads, n_meta=n_meta, lam_init=lam_init)
    t_p = bp * seq
    tm = _pick_tile(t_p, 512)
    h32, hb, gates = tail(x_prompt.reshape(t_p, d), mc.reshape(t_p, d), sga.reshape(t_p, d),
                          ya.reshape(t_p, d), tm=tm)
    y_prompt = moe(h32, hb, gates, tm=_pick_tile(t_p, 1024)).reshape(bp, seq, d)

    sk32, sv32, sqb, skb, svb, smc, ssga, snc = proj(
        x_sample, hist_rows(state_conv[0]), bb=bs, ts=s_len, transposed=False)
    sya = attn(sqb, cache_k[0].reshape(bs, n_meta + past, d), cache_v[0].reshape(bs, n_meta + past, d),
               skb, svb, tq=s_len, frame0=past)
    t_s = bs * s_len
    sh32, shb, sgates = tail(x_sample.reshape(t_s, d), smc.reshape(t_s, d), ssga.reshape(t_s, d),
                             sya.reshape(t_s, d), tm=t_s)
    y_sample = moe(sh32, shb, sgates, tm=t_s).reshape(bs, s_len, d)

    def with_meta(m, f):
        full = jnp.concatenate([jnp.broadcast_to(m, (bp, n_meta, d)), f], axis=1)
        return full.reshape(1, bp, n_meta + seq, n_heads, d // n_heads)

    return (y_prompt, y_sample,
            with_meta(mk32, k32), with_meta(mv32, v32), nc[None, :, HIST - 2:],
            sk32.reshape(1, bs, s_len, n_heads, d // n_heads),
            sv32.reshape(1, bs, s_len, n_heads, d // n_heads), snc[None, :, HIST - 2:])
```

```python
import functools
import math

import jax
import jax.numpy as jnp
from jax import lax
from jax.experimental import pallas as pl
from jax.experimental.pallas import tpu as pltpu

CHUNK = 64
CHUNK_SHIFT = CHUNK.bit_length() - 1
assert 1 << CHUNK_SHIFT == CHUNK
LANES = 128
BIAS_META, BIAS_DIAG = 0, 1
LN_EPS = 1e-5
RMS_EPS = 1e-5
HIST = 8
ROUTE_LANES = 128
ROUTE_E0, ROUTE_E1, ROUTE_W0, ROUTE_W1, ROUTE_R0, ROUTE_R1 = range(6)
TOP_K = 2
MOE_ROW_TILE = 256
VMEM_LIMIT = 52 * 1024 * 1024

F32 = jnp.float32
BF16 = jnp.bfloat16


def _const_spec(shape):
    return pl.BlockSpec(shape, lambda *_: (0,) * len(shape), pipeline_mode=pl.Buffered(1))


def _layer_norm(x, g, b):
    mu = jnp.mean(x, axis=-1, keepdims=True)
    xc = x - mu
    var = jnp.mean(xc * xc, axis=-1, keepdims=True)
    return xc * lax.rsqrt(var + LN_EPS) * g + b


def _proj_conv_kernel(x_ref, c0_ref, win_ref, cw_ref, wb0_ref, wqvt_ref,
                      k32_ref, v32_ref, q_ref, kb_ref, vb_ref, mc_ref, sga_ref, nc_ref,
                      carry_ref, *, bb, ts, d, q_scale, transposed):
    @pl.when(pl.program_id(1) == 0)
    def _():
        carry_ref[...] = c0_ref[...]

    xb = x_ref[...].reshape(bb * ts, d).astype(BF16)

    def proj(i):
        return jnp.dot(xb, win_ref[:, i * d:(i + 1) * d], preferred_element_type=F32)

    u = proj(2) * proj(0)
    cw = cw_ref[...]
    row = lax.broadcasted_iota(jnp.int32, (ts, d), 0)
    convs = []
    for b in range(bb):
        ub = u[b * ts:(b + 1) * ts]
        hist = carry_ref[b]
        h1 = hist[HIST - 1:HIST]
        h2 = hist[HIST - 2:HIST - 1]
        um1 = jnp.where(row == 0, h1, pltpu.roll(ub, 1, 0))
        um2 = jnp.where(row == 0, h2, jnp.where(row == 1, h1, pltpu.roll(ub, 2, 0)))
        convs.append(cw[0:1] * um2 + cw[1:2] * um1 + cw[2:3] * ub)
        carry_ref[b] = ub[ts - HIST:ts]
        nc_ref[b] = ub[ts - HIST:ts]
    conv = convs[0] if bb == 1 else jnp.concatenate(convs, axis=0)
    yc = (proj(1) * conv).astype(BF16)
    mc = jax.nn.sigmoid(proj(6)) * jnp.dot(yc, wb0_ref[...], preferred_element_type=F32)
    mc_ref[...] = mc.astype(BF16).reshape(bb, ts, d)
    sga_ref[...] = jax.nn.sigmoid(proj(7)).astype(BF16).reshape(bb, ts, d)
    k = proj(4)
    k32_ref[...] = k.reshape(bb, ts, d)
    kb_ref[...] = k.astype(BF16).reshape(bb, ts, d)
    v = proj(5)
    v32_ref[...] = v.reshape(bb, ts, d)
    if transposed:
        nt = (((1,), (1,)), ((), ()))
        qt = lax.dot_general(wqvt_ref[0:d, :], xb, nt, preferred_element_type=F32)
        q_ref[0, 0] = (qt * q_scale).astype(BF16)
        vb_ref[0, 0] = lax.dot_general(wqvt_ref[d:2 * d, :], xb, nt, preferred_element_type=F32).astype(BF16)
    else:
        q_ref[...] = (proj(3) * q_scale).astype(BF16).reshape(bb, ts, d)
        vb_ref[...] = v.astype(BF16).reshape(bb, ts, d)


def _proj_conv(x, c0, w_in, conv_w, wb0, wqvt, *, bb, ts, q_scale, transposed):
    nb, length, d = x.shape
    assert nb % bb == 0 and length % ts == 0 and ts % HIST == 0
    assert w_in.shape == (d, 8 * d), "all eight projection sections must be d_model wide"
    assert not transposed or bb == 1
    blk = pl.BlockSpec((bb, ts, d), lambda b, s: (b, s, 0))
    hist_blk = pl.BlockSpec((bb, HIST, d), lambda b, s: (b, 0, 0))
    f32_out = jax.ShapeDtypeStruct((nb, length, d), F32)
    bf_out = jax.ShapeDtypeStruct((nb, length, d), BF16)
    if transposed:
        qv_blk = pl.BlockSpec((1, 1, d, ts), lambda b, s: (b, s, 0, 0))
        qv_out = jax.ShapeDtypeStruct((nb, length // ts, d, ts), BF16)
    else:
        qv_blk, qv_out = blk, bf_out
    return pl.pallas_call(
        functools.partial(_proj_conv_kernel, bb=bb, ts=ts, d=d, q_scale=q_scale, transposed=transposed),
        grid=(nb // bb, length // ts),
        in_specs=[blk, hist_blk, _const_spec((d, 8 * d)), _const_spec((3, d)), _const_spec((d, d)),
                  _const_spec((2 * d, d))],
        out_specs=[blk, blk, qv_blk, blk, qv_blk, blk, blk, hist_blk],
        out_shape=[f32_out, f32_out, qv_out, bf_out, qv_out, bf_out, bf_out,
                   jax.ShapeDtypeStruct((nb, HIST, d), F32)],
        scratch_shapes=[pltpu.VMEM((bb, HIST, d), F32)],
        compiler_params=pltpu.CompilerParams(
            dimension_semantics=("parallel", "arbitrary"), vmem_limit_bytes=VMEM_LIMIT),
        name="proj_conv",
    )(x, c0, w_in, conv_w, wb0, wqvt)


def _pad_keys(x):
    pad = -x.shape[0] % LANES
    return x if pad == 0 else jnp.concatenate([x, jnp.zeros((pad, x.shape[1]), x.dtype)], axis=0)


def _attn_kernel(lqk_ref, g_ref, q_ref, ck_ref, cv_ref, kf_ref, vf_ref, o_ref,
                 *, tq, frame0, lam_init, dqk):
    qi = pl.program_id(2)
    lqk = lqk_ref[...]
    lam = (jnp.exp(jnp.sum(lqk[0:1] * lqk[1:2], axis=-1, keepdims=True))
           - jnp.exp(jnp.sum(lqk[2:3] * lqk[3:4], axis=-1, keepdims=True)) + lam_init)

    q = q_ref[0]
    lane = lax.broadcasted_iota(jnp.int32, q.shape, 1)
    zero = jnp.zeros_like(q)
    qs = jnp.concatenate([jnp.where(lane < dqk, q, zero), jnp.where(lane >= dqk, q, zero)], axis=0)

    def scores(k):
        return lax.dot_general(qs, k, (((1,), (1,)), ((), ())), preferred_element_type=F32)

    n_ctx = ck_ref.shape[1]
    s = scores(_pad_keys(ck_ref[0].astype(BF16)))
    if n_ctx % LANES:
        s = jnp.where(lax.broadcasted_iota(jnp.int32, s.shape, 1) < n_ctx, s, -jnp.inf)
    m = jnp.max(s, axis=-1, keepdims=True)
    p = jnp.exp(s - m)
    l = jnp.sum(p, axis=-1, keepdims=True)
    acc = jnp.dot(p.astype(BF16), _pad_keys(cv_ref[0].astype(BF16)), preferred_element_type=F32)

    def step(j, carry, masked):
        m, l, acc = carry
        start = pl.multiple_of(j * tq, tq)
        s = scores(_pad_keys(kf_ref[0, pl.ds(start, tq), :]))
        if masked:
            r = lax.broadcasted_iota(jnp.int32, s.shape, 0)
            c = lax.broadcasted_iota(jnp.int32, s.shape, 1)
            q_chunk = lax.shift_right_logical(frame0 + j * tq + jnp.where(r >= tq, r - tq, r), CHUNK_SHIFT)
            k_chunk = lax.shift_right_logical(frame0 + j * tq + c, CHUNK_SHIFT)
            s = jnp.where((k_chunk <= q_chunk) & (c < tq), s, -jnp.inf)
        m_new = jnp.maximum(m, jnp.max(s, axis=-1, keepdims=True))
        alpha = jnp.exp(m - m_new)
        p = jnp.exp(s - m_new)
        l = alpha * l + jnp.sum(p, axis=-1, keepdims=True)
        acc = alpha * acc + jnp.dot(p.astype(BF16), _pad_keys(vf_ref[0, pl.ds(start, tq), :]),
                                    preferred_element_type=F32)
        return m_new, l, acc

    carry = lax.fori_loop(0, qi, functools.partial(step, masked=False), (m, l, acc))
    m, l, acc = step(qi, carry, masked=True)

    o = acc[:tq] / l[:tq] - lam * (acc[tq:] / l[tq:])
    o = o * lax.rsqrt(jnp.mean(o * o, axis=-1, keepdims=True) + RMS_EPS) * g_ref[...] * (1.0 - lam_init)
    o_ref[0] = o.astype(o_ref.dtype)


def _attention(lqk, subln_g, q, ctx_k, ctx_v, kf, vf, *, n_heads, tq, frame0, lam_init):
    nb, length, d = q.shape
    dv = d // n_heads
    dqk = lqk.shape[1]
    assert dv == 2 * dqk and length % tq == 0
    assert tq % LANES == 0 or length == tq, "only a single (diagonal) frame tile may be lane-padded"
    n_ctx = ctx_k.shape[1]
    ctx_map = (lambda b, h, i: (b, 0, h)) if ctx_k.shape[0] == nb else (lambda b, h, i: (0, 0, h))
    q_blk = pl.BlockSpec((1, tq, dv), lambda b, h, i: (b, i, h))
    ctx_blk = pl.BlockSpec((1, n_ctx, dv), ctx_map)
    kv_blk = pl.BlockSpec((1, length, dv), lambda b, h, i: (b, 0, h))
    return pl.pallas_call(
        functools.partial(_attn_kernel, tq=tq, frame0=frame0, lam_init=lam_init, dqk=dqk),
        grid=(nb, n_heads, length // tq),
        in_specs=[pl.BlockSpec((4, dqk), lambda b, h, i: (0, 0)),
                  pl.BlockSpec((1, dv), lambda b, h, i: (0, 0)),
                  q_blk, ctx_blk, ctx_blk, kv_blk, kv_blk],
        out_specs=q_blk,
        out_shape=jax.ShapeDtypeStruct((nb, length, d), BF16),
        compiler_params=pltpu.CompilerParams(
            dimension_semantics=("parallel", "parallel", "arbitrary"), vmem_limit_bytes=VMEM_LIMIT),
        name="diff_attention",
    )(lqk, subln_g, q, ctx_k, ctx_v, kf, vf)


def _attn_t_kernel(lqk_ref, g_ref, bias_ref, qt_ref, mk_ref, mvt_ref, kf_ref, vt_ref, o_ref,
                   *, tq, n_tiles, lam_init, dqk):
    lqk = lqk_ref[...]
    lam = (jnp.exp(jnp.sum(lqk[0:1] * lqk[1:2], axis=-1, keepdims=True))
           - jnp.exp(jnp.sum(lqk[2:3] * lqk[3:4], axis=-1, keepdims=True)) + lam_init)

    qt = qt_ref[0, 0]
    row = lax.broadcasted_iota(jnp.int32, qt.shape, 0)
    zero = jnp.zeros_like(qt)
    w = jnp.concatenate([jnp.where(row < dqk, qt, zero), jnp.where(row >= dqk, qt, zero)], axis=1)

    def query_tile(q):
        n = q + 2

        def scores(t):
            if t == 0:
                return jnp.dot(mk_ref[...], w, preferred_element_type=F32) + bias_ref[BIAS_META]
            s = jnp.dot(kf_ref[0, (t - 1) * tq:t * tq, :], w, preferred_element_type=F32)
            return s + bias_ref[BIAS_DIAG] if t == n - 1 else s

        s = scores(0)
        for t in range(n):
            s_next = scores(t + 1) if t + 1 < n else None
            vt = mvt_ref[...] if t == 0 else vt_ref[0, t - 1]
            s_max = jnp.max(s, axis=0, keepdims=True)
            if t == 0:
                m = s_max
                p = jnp.exp(s - m)
                l = jnp.sum(p, axis=0, keepdims=True)
                acc = jnp.dot(vt, p.astype(BF16), preferred_element_type=F32)
            else:
                m_new = jnp.maximum(m, s_max)
                alpha = jnp.exp(m - m_new)
                p = jnp.exp(s - m_new)
                l = alpha * l + jnp.sum(p, axis=0, keepdims=True)
                acc = alpha * acc + jnp.dot(vt, p.astype(BF16), preferred_element_type=F32)
                m = m_new
            s = s_next

        ot = acc[:, :tq] / l[:, :tq] - lam * (acc[:, tq:] / l[:, tq:])
        ot = ot * lax.rsqrt(jnp.mean(ot * ot, axis=0, keepdims=True) + RMS_EPS)
        o_ref[0] = (ot.T * g_ref[...] * (1.0 - lam_init)).astype(o_ref.dtype)

    for q in range(n_tiles):
        pl.when(pl.program_id(2) == q)(functools.partial(query_tile, q))


def _attention_bias(tq, n_meta):
    r = jnp.arange(tq, dtype=jnp.int32)[:, None]
    c = jnp.arange(2 * tq, dtype=jnp.int32)[None, :] % tq
    meta = jnp.broadcast_to(r < n_meta, (tq, 2 * tq))
    diag = (r // CHUNK) <= (c // CHUNK)
    neg = jnp.float32(-jnp.inf)
    return jnp.stack([jnp.where(meta, 0.0, neg), jnp.where(diag, 0.0, neg)])


def _attention_t(lqk, subln_g, qt, mk, mvt, kf, vt, *, n_heads, n_meta, lam_init):
    nb, n_tiles, d, tq = qt.shape
    length = n_tiles * tq
    dv = d // n_heads
    dqk = lqk.shape[1]
    assert dv == 2 * dqk and kf.shape == (nb, length, d) and vt.shape == qt.shape
    assert mk.shape == (tq, d) and mvt.shape == (d, tq) and n_meta <= tq and tq % CHUNK == 0
    return pl.pallas_call(
        functools.partial(_attn_t_kernel, tq=tq, n_tiles=n_tiles, lam_init=lam_init, dqk=dqk),
        grid=(nb, n_heads, n_tiles),
        in_specs=[pl.BlockSpec((4, dqk), lambda b, h, i: (0, 0)),
                  pl.BlockSpec((1, dv), lambda b, h, i: (0, 0)),
                  _const_spec((2, tq, 2 * tq)),
                  pl.BlockSpec((1, 1, dv, tq), lambda b, h, i: (b, i, h, 0)),
                  pl.BlockSpec((tq, dv), lambda b, h, i: (0, h)),
                  pl.BlockSpec((dv, tq), lambda b, h, i: (h, 0)),
                  pl.BlockSpec((1, length, dv), lambda b, h, i: (b, 0, h)),
                  pl.BlockSpec((1, n_tiles, dv, tq), lambda b, h, i: (b, 0, h, 0))],
        out_specs=pl.BlockSpec((1, tq, dv), lambda b, h, i: (b, i, h)),
        out_shape=jax.ShapeDtypeStruct((nb, length, d), BF16),
        compiler_params=pltpu.CompilerParams(
            dimension_semantics=("parallel", "parallel", "arbitrary"), vmem_limit_bytes=VMEM_LIMIT),
        name="diff_attention_t",
    )(lqk, subln_g, _attention_bias(tq, n_meta), qt, mk, mvt, kf, vt)


def _tail_kernel(x_ref, mc_ref, sga_ref, ya_ref, wb1_ref, wout_ref, g_ref, b_ref, wrt_ref, brt_ref,
                 h32_ref, route_ref, counts_ref, *, alpha, n_experts, n_groups):
    @pl.when(pl.program_id(0) == 0)
    def _():
        counts_ref[...] = jnp.zeros_like(counts_ref)

    merged = mc_ref[...].astype(F32) + sga_ref[...].astype(F32) * jnp.dot(
        ya_ref[...], wb1_ref[...], preferred_element_type=F32)
    y = alpha * x_ref[...] + jnp.dot(merged.astype(BF16), wout_ref[...], preferred_element_type=F32)
    h = _layer_norm(y, g_ref[...], b_ref[...])
    h32_ref[...] = h

    logits = jnp.dot(h, wrt_ref[...], preferred_element_type=F32,
                     precision=lax.Precision.HIGHEST) + brt_ref[...]
    lane = lax.broadcasted_iota(jnp.int32, logits.shape, 1)
    big = jnp.int32(ROUTE_LANES)
    neg = -jnp.inf
    per_group = n_experts // n_groups

    def first_max(v):
        vmax = jnp.max(v, axis=-1, keepdims=True)
        return vmax, jnp.min(jnp.where(v == vmax, lane, big), axis=-1, keepdims=True)

    is_group = (lane >= n_experts) & (lane < n_experts + n_groups)
    lg = jnp.where(is_group, logits, neg)
    eg = jnp.exp(lg - jnp.max(lg, axis=-1, keepdims=True))
    p_group = jnp.where(is_group, eg / jnp.sum(eg, axis=-1, keepdims=True), neg)
    p_g, g_lane = first_max(p_group)
    g_idx = g_lane - n_experts
    lo = g_idx * per_group
    le = jnp.where((lane >= lo) & (lane < lo + per_group), logits, neg)
    v1, i1 = first_max(le)
    v2, i2 = first_max(jnp.where(lane == i1, neg, le))
    e2 = jnp.exp(v2 - v1)
    w1 = p_g / (1.0 + e2)
    w2 = p_g * e2 / (1.0 + e2)

    tm = logits.shape[0]
    onehot = jnp.where((lane == i1) | (lane == i2), 1.0, 0.0).astype(BF16)
    r = lax.broadcasted_iota(jnp.int32, (tm, tm), 0)
    c = lax.broadcasted_iota(jnp.int32, (tm, tm), 1)
    earlier = jnp.where(c < r, 1.0, 0.0).astype(BF16)
    before = jnp.dot(earlier, onehot, preferred_element_type=F32) + counts_ref[...]
    r1 = jnp.sum(jnp.where(lane == i1, before, 0.0), axis=-1, keepdims=True)
    r2 = jnp.sum(jnp.where(lane == i2, before, 0.0), axis=-1, keepdims=True)
    counts_ref[...] += jnp.sum(onehot.astype(F32), axis=0, keepdims=True)

    route = jnp.zeros_like(logits)
    for k, col in ((ROUTE_E0, i1.astype(F32)), (ROUTE_E1, i2.astype(F32)), (ROUTE_W0, w1), (ROUTE_W1, w2),
                   (ROUTE_R0, r1), (ROUTE_R1, r2)):
        route = jnp.where(lane == k, col, route)
    route_ref[...] = route


def _tail(x, mc, sga, ya, wb1, w_out, ln_g, ln_b, w_rt, b_rt, *, tm, alpha, n_experts, n_groups):
    t, d = x.shape
    assert t % tm == 0
    row = lambda w: pl.BlockSpec((tm, w), lambda i: (i, 0))
    return pl.pallas_call(
        functools.partial(_tail_kernel, alpha=alpha, n_experts=n_experts, n_groups=n_groups),
        grid=(t // tm,),
        in_specs=[row(d), row(d), row(d), row(d), _const_spec((d, d)), _const_spec((d, d)),
                  _const_spec((1, d)), _const_spec((1, d)),
                  _const_spec((d, ROUTE_LANES)), _const_spec((1, ROUTE_LANES))],
        out_specs=[row(d), row(ROUTE_LANES), pl.BlockSpec((1, ROUTE_LANES), lambda i: (0, 0))],
        out_shape=[jax.ShapeDtypeStruct((t, d), F32), jax.ShapeDtypeStruct((t, ROUTE_LANES), F32),
                   jax.ShapeDtypeStruct((1, ROUTE_LANES), F32)],
        compiler_params=pltpu.CompilerParams(
            dimension_semantics=("arbitrary",), vmem_limit_bytes=VMEM_LIMIT),
        name="merge_ln_router",
    )(x, mc, sga, ya, wb1, w_out, ln_g, ln_b, w_rt, b_rt)


def _row_copy(src_ref, src_row, dst_ref, dst_row, sem):
    return pltpu.make_async_copy(src_ref.at[pl.ds(src_row, 1)], dst_ref.at[pl.ds(dst_row, 1)], sem)


def _moe_scatter_kernel(pos_ref, h_ref, xs_in_ref, xs_ref, sem, *, tm):
    del xs_in_ref

    def issue(t, _):
        for k in range(TOP_K):
            _row_copy(h_ref, t, xs_ref, pos_ref[0, 0, k * tm + t], sem).start()
        return 0

    lax.fori_loop(0, tm, issue, 0)
    for _ in range(TOP_K):
        pltpu.make_async_copy(h_ref, xs_ref.at[pl.ds(0, tm)], sem).wait()


def _moe_scatter(pos, h32, n_rows, *, tm):
    t, d = h32.shape
    return pl.pallas_call(
        functools.partial(_moe_scatter_kernel, tm=tm),
        grid=(t // tm,),
        in_specs=[pl.BlockSpec((1, 1, TOP_K * tm), lambda i: (i, 0, 0), memory_space=pltpu.SMEM),
                  pl.BlockSpec((tm, d), lambda i: (i, 0)),
                  pl.BlockSpec(memory_space=pl.ANY)],
        out_specs=pl.BlockSpec(memory_space=pl.ANY),
        out_shape=jax.ShapeDtypeStruct((n_rows, d), F32),
        scratch_shapes=[pltpu.SemaphoreType.DMA(())],
        input_output_aliases={2: 0},
        compiler_params=pltpu.CompilerParams(
            dimension_semantics=("arbitrary",), vmem_limit_bytes=VMEM_LIMIT, has_side_effects=True),
        name="moe_scatter",
    )(pos, h32, jnp.zeros((n_rows, d), F32))


def _moe_experts_kernel(tile_expert_ref, n_used_ref, xs_ref, wgu_ref, wd_ref, ys_ref, *, d_expert):
    del tile_expert_ref
    used = pl.program_id(0) < n_used_ref[0]

    @pl.when(used)
    def _():
        ab = jnp.dot(xs_ref[...].astype(BF16), wgu_ref[0], preferred_element_type=F32)
        a = ab[:, :d_expert]
        act = (a * jax.nn.sigmoid(a) * ab[:, d_expert:]).astype(BF16)
        ys_ref[...] = jnp.dot(act, wd_ref[0], preferred_element_type=F32)

    @pl.when(jnp.logical_not(used))
    def _():
        ys_ref[...] = jnp.zeros_like(ys_ref)


def _moe_experts(tile_expert, n_used, xs, wgu, wd, *, tr):
    n_rows, d = xs.shape
    _, _, two_de = wgu.shape
    return pl.pallas_call(
        functools.partial(_moe_experts_kernel, d_expert=two_de // 2),
        grid_spec=pltpu.PrefetchScalarGridSpec(
            num_scalar_prefetch=2,
            grid=(n_rows // tr,),
            in_specs=[pl.BlockSpec((tr, d), lambda r, te, nu: (r, 0)),
                      pl.BlockSpec((1, d, two_de), lambda r, te, nu: (te[r], 0, 0)),
                      pl.BlockSpec((1, two_de // 2, d), lambda r, te, nu: (te[r], 0, 0))],
            out_specs=pl.BlockSpec((tr, d), lambda r, te, nu: (r, 0))),
        out_shape=jax.ShapeDtypeStruct((n_rows, d), F32),
        compiler_params=pltpu.CompilerParams(
            dimension_semantics=("arbitrary",), vmem_limit_bytes=VMEM_LIMIT),
        name="moe_experts",
    )(tile_expert, n_used, xs, wgu, wd)


def _moe_combine_kernel(pos_ref, route_ref, h_ref, g_ref, b_ref, ys_ref, o_ref, y_buf, sem, *, tm, alpha):
    def issue(t, _):
        for k in range(TOP_K):
            _row_copy(ys_ref, pos_ref[0, 0, k * tm + t], y_buf.at[k], t, sem).start()
        return 0

    lax.fori_loop(0, tm, issue, 0)
    for k in range(TOP_K):
        pltpu.make_async_copy(ys_ref.at[pl.ds(0, tm)], y_buf.at[k], sem).wait()

    route = route_ref[...]
    lane = lax.broadcasted_iota(jnp.int32, route.shape, 1)
    f = jnp.zeros_like(h_ref)
    for k, w_lane in enumerate((ROUTE_W0, ROUTE_W1)):
        f = f + jnp.sum(jnp.where(lane == w_lane, route, 0.0), axis=-1, keepdims=True) * y_buf[k]
    o_ref[...] = _layer_norm(alpha * h_ref[...] + f, g_ref[...], b_ref[...])


def _moe_combine(pos, route, h32, ys, ln_g, ln_b, *, tm, alpha):
    t, d = h32.shape
    return pl.pallas_call(
        functools.partial(_moe_combine_kernel, tm=tm, alpha=alpha),
        grid=(t // tm,),
        in_specs=[pl.BlockSpec((1, 1, TOP_K * tm), lambda i: (i, 0, 0), memory_space=pltpu.SMEM),
                  pl.BlockSpec((tm, ROUTE_LANES), lambda i: (i, 0)),
                  pl.BlockSpec((tm, d), lambda i: (i, 0)),
                  _const_spec((1, d)), _const_spec((1, d)),
                  pl.BlockSpec(memory_space=pl.ANY)],
        out_specs=pl.BlockSpec((tm, d), lambda i: (i, 0)),
        out_shape=jax.ShapeDtypeStruct((t, d), F32),
        scratch_shapes=[pltpu.VMEM((TOP_K, tm, d), F32), pltpu.SemaphoreType.DMA(())],
        compiler_params=pltpu.CompilerParams(
            dimension_semantics=("arbitrary",), vmem_limit_bytes=VMEM_LIMIT),
        name="moe_combine",
    )(pos, route, h32, ln_g, ln_b, ys)


def _moe(h32, route, counts, wgu, wd, ln_g, ln_b, *, tm, alpha):
    t, d = h32.shape
    n_experts = wgu.shape[0]
    tr = MOE_ROW_TILE
    assert t % tm == 0
    n_tiles = (TOP_K * t + n_experts * (tr - 1)) // tr

    cnt = counts[0, :n_experts].astype(jnp.int32)
    padded = (cnt + tr - 1) // tr * tr
    ends = jnp.cumsum(padded)
    expert = route[:, ROUTE_E0:ROUTE_E1 + 1].astype(jnp.int32)
    rank = route[:, ROUTE_R0:ROUTE_R1 + 1].astype(jnp.int32)
    pos = (ends - padded)[expert] + rank
    pos = pos.reshape(t // tm, tm, TOP_K).transpose(0, 2, 1).reshape(t // tm, 1, TOP_K * tm)
    tile_expert = jnp.minimum(jnp.searchsorted(ends, jnp.arange(n_tiles, dtype=jnp.int32) * tr, side="right"),
                              n_experts - 1).astype(jnp.int32)
    n_used = (ends[-1:] // tr).astype(jnp.int32)

    xs = _moe_scatter(pos, h32, n_tiles * tr, tm=tm)
    ys = _moe_experts(tile_expert, n_used, xs, wgu, wd, tr=tr)
    return _moe_combine(pos, route, h32, ys, ln_g, ln_b, tm=tm, alpha=alpha)


def _pick_tile(n, target):
    t = min(n, target)
    while n % t:
        t //= 2
    return t


def kernel(x_prompt, x_sample, cache_k, cache_v, state_conv, meta_tokens, w_in, conv_w, lambda_qk, subln_g, w_branch, w_out, ln1_g, ln1_b, w_group, b_group, w_router, b_router, w_gate_up, w_down, ln2_g, ln2_b):
    depth = w_in.shape[0]
    assert depth == 1, "single-layer step only"
    bp, seq, d = x_prompt.shape
    bs, s_len, _ = x_sample.shape
    n_meta = meta_tokens.shape[0]
    n_heads = cache_k.shape[3]
    dqk = cache_k.shape[4] // 2
    past = cache_k.shape[2] - n_meta
    n_groups = w_group.shape[-1]
    n_experts = w_router.shape[-1]
    assert n_experts + n_groups <= ROUTE_LANES
    q_scale = dqk ** -0.5
    assert math.frexp(q_scale)[0] == 0.5, "the score scale is folded into q; exact only for powers of two"
    alpha = (2.0 * depth) ** 0.25
    lam_init = 0.8 - 0.6 * math.exp(-0.3 * 0)

    w_in_b = w_in[0].astype(BF16)
    wb = w_branch[0].astype(BF16)
    w_out_b = w_out[0].astype(BF16)
    wgu_b = w_gate_up[0].astype(BF16)
    wd_b = w_down[0].astype(BF16)
    w_rt = jnp.zeros((d, ROUTE_LANES), F32).at[:, :n_experts].set(w_router[0]).at[
        :, n_experts:n_experts + n_groups].set(w_group[0])
    b_rt = jnp.zeros((1, ROUTE_LANES), F32).at[0, :n_experts].set(b_router[0]).at[
        0, n_experts:n_experts + n_groups].set(b_group[0])

    def hist_rows(rows):
        return jnp.pad(rows, ((0, 0), (HIST - rows.shape[1], 0), (0, 0)))

    wqvt = jnp.concatenate([w_in_b[:, 3 * d:4 * d].T, w_in_b[:, 5 * d:6 * d].T], axis=0)
    proj = functools.partial(_proj_conv, w_in=w_in_b, conv_w=conv_w[0], wb0=wb[0], wqvt=wqvt, q_scale=q_scale)
    attn = functools.partial(_attention, lambda_qk[0], subln_g, n_heads=n_heads, lam_init=lam_init)
    tail = functools.partial(_tail, wb1=wb[1], w_out=w_out_b, ln_g=ln1_g, ln_b=ln1_b, w_rt=w_rt, b_rt=b_rt,
                             alpha=alpha, n_experts=n_experts, n_groups=n_groups)
    moe = functools.partial(_moe, wgu=wgu_b, wd=wd_b, ln_g=ln2_g, ln_b=ln2_b, alpha=alpha)

    mk32, mv32, _, mkb, mvb, _, _, mnc = proj(
        meta_tokens[None], jnp.zeros((1, HIST, d), F32), bb=1, ts=n_meta, transposed=False)
    ts = _pick_tile(seq, 256)
    mk_pad = jnp.pad(mkb[0], ((0, ts - n_meta), (0, 0)))
    mvt_pad = jnp.pad(mvb[0].T, ((0, 0), (0, ts - n_meta)))
    k32, v32, qt, kb, vt, mc, sga, nc = proj(
        x_prompt, jnp.broadcast_to(mnc, (bp, HIST, d)), bb=1, ts=ts, transposed=True)
    ya = _attention_t(lambda_qk[0], subln_g, qt, mk_pad, mvt_pad, kb, vt,
                      n_heads=n_heads, n_meta=n_meta, lam_init=lam_init)
    t_p = bp * seq
    tm = _pick_tile(t_p, 512)
    h32, route, counts = tail(x_prompt.reshape(t_p, d), mc.reshape(t_p, d), sga.reshape(t_p, d),
                              ya.reshape(t_p, d), tm=tm)
    y_prompt = moe(h32, route, counts, tm=tm).reshape(bp, seq, d)

    sk32, sv32, sqb, skb, svb, smc, ssga, snc = proj(
        x_sample, hist_rows(state_conv[0]), bb=bs, ts=s_len, transposed=False)
    sya = attn(sqb, cache_k[0].reshape(bs, n_meta + past, d), cache_v[0].reshape(bs, n_meta + past, d),
               skb, svb, tq=s_len, frame0=past)
    t_s = bs * s_len
    sh32, sroute, scounts = tail(x_sample.reshape(t_s, d), smc.reshape(t_s, d), ssga.reshape(t_s, d),
                                 sya.reshape(t_s, d), tm=t_s)
    y_sample = moe(sh32, sroute, scounts, tm=t_s).reshape(bs, s_len, d)

    def with_meta(m, f):
        full = jnp.concatenate([jnp.broadcast_to(m, (bp, n_meta, d)), f], axis=1)
        return full.reshape(1, bp, n_meta + seq, n_heads, d // n_heads)

    return (y_prompt, y_sample,
            with_meta(mk32, k32), with_meta(mv32, v32), nc[None, :, HIST - 2:],
            sk32.reshape(1, bs, s_len, n_heads, d // n_heads),
            sv32.reshape(1, bs, s_len, n_heads, d // n_heads), snc[None, :, HIST - 2:])
```

```python
import functools
import math

import jax
import jax.numpy as jnp
from jax import lax
from jax.experimental import pallas as pl
from jax.experimental.pallas import tpu as pltpu

CHUNK = 64
CHUNK_SHIFT = CHUNK.bit_length() - 1
assert 1 << CHUNK_SHIFT == CHUNK
LANES = 128
BF16_SUBLANES = 16
LOG2_E = math.log2(math.e)
BIAS_META, BIAS_DIAG = 0, 1
LN_EPS = 1e-5
RMS_EPS = 1e-5
HIST = 8
ROUTE_LANES = 128
ROUTE_E0, ROUTE_E1, ROUTE_W0, ROUTE_W1, ROUTE_R0, ROUTE_R1 = range(6)
TOP_K = 2
MOE_ROW_TILE = 256
VMEM_LIMIT = 52 * 1024 * 1024

F32 = jnp.float32
BF16 = jnp.bfloat16


def _const_spec(shape):
    return pl.BlockSpec(shape, lambda *_: (0,) * len(shape), pipeline_mode=pl.Buffered(1))


def _layer_norm(x, g, b):
    mu = jnp.mean(x, axis=-1, keepdims=True)
    xc = x - mu
    var = jnp.mean(xc * xc, axis=-1, keepdims=True)
    return xc * lax.rsqrt(var + LN_EPS) * g + b


def _pack_bf16_pairs(x):
    half = x.shape[1] // 2
    bits = pltpu.bitcast(x.astype(BF16).astype(F32), jnp.uint32)
    return bits[:, half:] | (bits[:, :half] >> 16)


def _unpack_bf16_pairs(w):
    lo = pltpu.bitcast(w << 16, F32)
    hi = pltpu.bitcast(w & jnp.uint32(0xFFFF0000), F32)
    return lo, hi


def _proj_conv_kernel(x_ref, c0_ref, win_ref, cw_ref, wb0_ref, wqvt_ref,
                      k32_ref, v32_ref, q_ref, kb_ref, vb_ref, mc_ref, sga_ref, nc_ref,
                      carry_ref, *, bb, ts, d, q_scale, transposed):
    @pl.when(pl.program_id(1) == 0)
    def _():
        carry_ref[...] = c0_ref[...]

    xb = x_ref[...].reshape(bb * ts, d).astype(BF16)

    def proj(i):
        return jnp.dot(xb, win_ref[:, i * d:(i + 1) * d], preferred_element_type=F32)

    u = proj(2) * proj(0)
    cw = cw_ref[...]
    row = lax.broadcasted_iota(jnp.int32, (ts, d), 0)
    convs = []
    for b in range(bb):
        ub = u[b * ts:(b + 1) * ts]
        hist = carry_ref[b]
        h1 = hist[HIST - 1:HIST]
        h2 = hist[HIST - 2:HIST - 1]
        um1 = jnp.where(row == 0, h1, pltpu.roll(ub, 1, 0))
        um2 = jnp.where(row == 0, h2, jnp.where(row == 1, h1, pltpu.roll(ub, 2, 0)))
        convs.append(cw[0:1] * um2 + cw[1:2] * um1 + cw[2:3] * ub)
        carry_ref[b] = ub[ts - HIST:ts]
        nc_ref[b] = ub[ts - HIST:ts]
    conv = convs[0] if bb == 1 else jnp.concatenate(convs, axis=0)
    yc = (proj(1) * conv).astype(BF16)
    mc = jax.nn.sigmoid(proj(6)) * jnp.dot(yc, wb0_ref[...], preferred_element_type=F32)
    mc_ref[...] = mc.astype(BF16).reshape(bb, ts, d)
    sga_ref[...] = jax.nn.sigmoid(proj(7)).astype(BF16).reshape(bb, ts, d)
    k = proj(4)
    k32_ref[...] = k.reshape(bb, ts, d)
    kb_ref[...] = k.astype(BF16).reshape(bb, ts, d)
    v = proj(5)
    v32_ref[...] = v.reshape(bb, ts, d)
    if transposed:
        nt = (((1,), (1,)), ((), ()))
        qt = lax.dot_general(wqvt_ref[0:d, :], xb, nt, preferred_element_type=F32)
        q_ref[0, 0] = (qt * (q_scale * LOG2_E)).astype(BF16)
        vb_ref[0, 0] = lax.dot_general(wqvt_ref[d:2 * d, :], xb, nt, preferred_element_type=F32).astype(BF16)
    else:
        q_ref[...] = (proj(3) * q_scale).astype(BF16).reshape(bb, ts, d)
        vb_ref[...] = v.astype(BF16).reshape(bb, ts, d)


def _proj_conv(x, c0, w_in, conv_w, wb0, wqvt, *, bb, ts, q_scale, transposed):
    nb, length, d = x.shape
    assert nb % bb == 0 and length % ts == 0 and ts % HIST == 0
    assert w_in.shape == (d, 8 * d), "all eight projection sections must be d_model wide"
    assert not transposed or bb == 1
    blk = pl.BlockSpec((bb, ts, d), lambda b, s: (b, s, 0))
    hist_blk = pl.BlockSpec((bb, HIST, d), lambda b, s: (b, 0, 0))
    f32_out = jax.ShapeDtypeStruct((nb, length, d), F32)
    bf_out = jax.ShapeDtypeStruct((nb, length, d), BF16)
    if transposed:
        qv_blk = pl.BlockSpec((1, 1, d, ts), lambda b, s: (b, s, 0, 0))
        qv_out = jax.ShapeDtypeStruct((nb, length // ts, d, ts), BF16)
    else:
        qv_blk, qv_out = blk, bf_out
    return pl.pallas_call(
        functools.partial(_proj_conv_kernel, bb=bb, ts=ts, d=d, q_scale=q_scale, transposed=transposed),
        grid=(nb // bb, length // ts),
        in_specs=[blk, hist_blk, _const_spec((d, 8 * d)), _const_spec((3, d)), _const_spec((d, d)),
                  _const_spec((2 * d, d))],
        out_specs=[blk, blk, qv_blk, blk, qv_blk, blk, blk, hist_blk],
        out_shape=[f32_out, f32_out, qv_out, bf_out, qv_out, bf_out, bf_out,
                   jax.ShapeDtypeStruct((nb, HIST, d), F32)],
        scratch_shapes=[pltpu.VMEM((bb, HIST, d), F32)],
        compiler_params=pltpu.CompilerParams(
            dimension_semantics=("parallel", "arbitrary"), vmem_limit_bytes=VMEM_LIMIT),
        name="proj_conv",
    )(x, c0, w_in, conv_w, wb0, wqvt)


def _pad_keys(x):
    pad = -x.shape[0] % LANES
    return x if pad == 0 else jnp.concatenate([x, jnp.zeros((pad, x.shape[1]), x.dtype)], axis=0)


def _attn_kernel(lqk_ref, g_ref, q_ref, ck_ref, cv_ref, kf_ref, vf_ref, o_ref,
                 *, tq, frame0, lam_init, dqk):
    qi = pl.program_id(2)
    lqk = lqk_ref[...]
    lam = (jnp.exp(jnp.sum(lqk[0:1] * lqk[1:2], axis=-1, keepdims=True))
           - jnp.exp(jnp.sum(lqk[2:3] * lqk[3:4], axis=-1, keepdims=True)) + lam_init)

    q = q_ref[0]
    lane = lax.broadcasted_iota(jnp.int32, q.shape, 1)
    zero = jnp.zeros_like(q)
    qs = jnp.concatenate([jnp.where(lane < dqk, q, zero), jnp.where(lane >= dqk, q, zero)], axis=0)

    def scores(k):
        return lax.dot_general(qs, k, (((1,), (1,)), ((), ())), preferred_element_type=F32)

    n_ctx = ck_ref.shape[1]
    s = scores(_pad_keys(ck_ref[0].astype(BF16)))
    if n_ctx % LANES:
        s = jnp.where(lax.broadcasted_iota(jnp.int32, s.shape, 1) < n_ctx, s, -jnp.inf)
    m = jnp.max(s, axis=-1, keepdims=True)
    p = jnp.exp(s - m)
    l = jnp.sum(p, axis=-1, keepdims=True)
    acc = jnp.dot(p.astype(BF16), _pad_keys(cv_ref[0].astype(BF16)), preferred_element_type=F32)

    def step(j, carry, masked):
        m, l, acc = carry
        start = pl.multiple_of(j * tq, tq)
        s = scores(_pad_keys(kf_ref[0, pl.ds(start, tq), :]))
        if masked:
            r = lax.broadcasted_iota(jnp.int32, s.shape, 0)
            c = lax.broadcasted_iota(jnp.int32, s.shape, 1)
            q_chunk = lax.shift_right_logical(frame0 + j * tq + jnp.where(r >= tq, r - tq, r), CHUNK_SHIFT)
            k_chunk = lax.shift_right_logical(frame0 + j * tq + c, CHUNK_SHIFT)
            s = jnp.where((k_chunk <= q_chunk) & (c < tq), s, -jnp.inf)
        m_new = jnp.maximum(m, jnp.max(s, axis=-1, keepdims=True))
        alpha = jnp.exp(m - m_new)
        p = jnp.exp(s - m_new)
        l = alpha * l + jnp.sum(p, axis=-1, keepdims=True)
        acc = alpha * acc + jnp.dot(p.astype(BF16), _pad_keys(vf_ref[0, pl.ds(start, tq), :]),
                                    preferred_element_type=F32)
        return m_new, l, acc

    carry = lax.fori_loop(0, qi, functools.partial(step, masked=False), (m, l, acc))
    m, l, acc = step(qi, carry, masked=True)

    o = acc[:tq] / l[:tq] - lam * (acc[tq:] / l[tq:])
    o = o * lax.rsqrt(jnp.mean(o * o, axis=-1, keepdims=True) + RMS_EPS) * g_ref[...] * (1.0 - lam_init)
    o_ref[0] = o.astype(o_ref.dtype)


def _attention(lqk, subln_g, q, ctx_k, ctx_v, kf, vf, *, n_heads, tq, frame0, lam_init):
    nb, length, d = q.shape
    dv = d // n_heads
    dqk = lqk.shape[1]
    assert dv == 2 * dqk and length % tq == 0
    assert tq % LANES == 0 or length == tq, "only a single (diagonal) frame tile may be lane-padded"
    n_ctx = ctx_k.shape[1]
    ctx_map = (lambda b, h, i: (b, 0, h)) if ctx_k.shape[0] == nb else (lambda b, h, i: (0, 0, h))
    q_blk = pl.BlockSpec((1, tq, dv), lambda b, h, i: (b, i, h))
    ctx_blk = pl.BlockSpec((1, n_ctx, dv), ctx_map)
    kv_blk = pl.BlockSpec((1, length, dv), lambda b, h, i: (b, 0, h))
    return pl.pallas_call(
        functools.partial(_attn_kernel, tq=tq, frame0=frame0, lam_init=lam_init, dqk=dqk),
        grid=(nb, n_heads, length // tq),
        in_specs=[pl.BlockSpec((4, dqk), lambda b, h, i: (0, 0)),
                  pl.BlockSpec((1, dv), lambda b, h, i: (0, 0)),
                  q_blk, ctx_blk, ctx_blk, kv_blk, kv_blk],
        out_specs=q_blk,
        out_shape=jax.ShapeDtypeStruct((nb, length, d), BF16),
        compiler_params=pltpu.CompilerParams(
            dimension_semantics=("parallel", "parallel", "arbitrary"), vmem_limit_bytes=VMEM_LIMIT),
        name="diff_attention",
    )(lqk, subln_g, q, ctx_k, ctx_v, kf, vf)


def _attn_t_kernel(lqk_ref, g_ref, bias_ref, qt_ref, mk_ref, mvt_ref, kf_ref, vt_ref, o_ref,
                   *, tq, n_tiles, lam_init, dqk):
    lqk = lqk_ref[...]
    lam = (jnp.exp(jnp.sum(lqk[0:1] * lqk[1:2], axis=-1, keepdims=True))
           - jnp.exp(jnp.sum(lqk[2:3] * lqk[3:4], axis=-1, keepdims=True)) + lam_init)

    qt = qt_ref[0, 0]
    row = lax.broadcasted_iota(jnp.int32, qt.shape, 0)
    zero = jnp.zeros_like(qt)
    w = jnp.concatenate([jnp.where(row < dqk, qt, zero), jnp.where(row >= dqk, qt, zero)], axis=1)

    dv = qt.shape[0]
    ones = jnp.ones((BF16_SUBLANES, tq), BF16)

    def query_tile(q):
        n = q + 2

        def scores(t):
            if t == 0:
                return jnp.dot(mk_ref[...], w, preferred_element_type=F32) + bias_ref[BIAS_META]
            s = jnp.dot(kf_ref[0, (t - 1) * tq:t * tq, :], w, preferred_element_type=F32)
            return s + bias_ref[BIAS_DIAG] if t == n - 1 else s

        s = scores(0)
        for t in range(n):
            s_next = scores(t + 1) if t + 1 < n else None
            vt = jnp.concatenate([mvt_ref[...] if t == 0 else vt_ref[0, t - 1], ones], axis=0)
            s_max = jnp.max(s, axis=0, keepdims=True)
            if t == 0:
                m = s_max
                acc = jnp.dot(vt, jnp.exp2(s - m).astype(BF16), preferred_element_type=F32)
            else:
                m_new = jnp.maximum(m, s_max)
                acc = jnp.exp2(m - m_new) * acc + jnp.dot(vt, jnp.exp2(s - m_new).astype(BF16),
                                                          preferred_element_type=F32)
                m = m_new
            s = s_next

        l = acc[dv:dv + 1]
        ot = acc[:dv, :tq] / l[:, :tq] - lam * (acc[:dv, tq:] / l[:, tq:])
        ot = ot * lax.rsqrt(jnp.mean(ot * ot, axis=0, keepdims=True) + RMS_EPS)
        o_ref[0] = (ot.T * g_ref[...] * (1.0 - lam_init)).astype(o_ref.dtype)

    for q in range(n_tiles):
        pl.when(pl.program_id(2) == q)(functools.partial(query_tile, q))


def _attention_bias(tq, n_meta):
    r = jnp.arange(tq, dtype=jnp.int32)[:, None]
    c = jnp.arange(2 * tq, dtype=jnp.int32)[None, :] % tq
    meta = jnp.broadcast_to(r < n_meta, (tq, 2 * tq))
    diag = (r // CHUNK) <= (c // CHUNK)
    neg = jnp.float32(-jnp.inf)
    return jnp.stack([jnp.where(meta, 0.0, neg), jnp.where(diag, 0.0, neg)])


def _attention_t(lqk, subln_g, qt, mk, mvt, kf, vt, *, n_heads, n_meta, lam_init):
    nb, n_tiles, d, tq = qt.shape
    length = n_tiles * tq
    dv = d // n_heads
    dqk = lqk.shape[1]
    assert dv == 2 * dqk and kf.shape == (nb, length, d) and vt.shape == qt.shape
    assert mk.shape == (tq, d) and mvt.shape == (d, tq) and n_meta <= tq and tq % CHUNK == 0
    return pl.pallas_call(
        functools.partial(_attn_t_kernel, tq=tq, n_tiles=n_tiles, lam_init=lam_init, dqk=dqk),
        grid=(nb, n_heads, n_tiles),
        in_specs=[pl.BlockSpec((4, dqk), lambda b, h, i: (0, 0)),
                  pl.BlockSpec((1, dv), lambda b, h, i: (0, 0)),
                  _const_spec((2, tq, 2 * tq)),
                  pl.BlockSpec((1, 1, dv, tq), lambda b, h, i: (b, i, h, 0)),
                  pl.BlockSpec((tq, dv), lambda b, h, i: (0, h)),
                  pl.BlockSpec((dv, tq), lambda b, h, i: (h, 0)),
                  pl.BlockSpec((1, length, dv), lambda b, h, i: (b, 0, h)),
                  pl.BlockSpec((1, n_tiles, dv, tq), lambda b, h, i: (b, 0, h, 0))],
        out_specs=pl.BlockSpec((1, tq, dv), lambda b, h, i: (b, i, h)),
        out_shape=jax.ShapeDtypeStruct((nb, length, d), BF16),
        compiler_params=pltpu.CompilerParams(
            dimension_semantics=("parallel", "parallel", "arbitrary"), vmem_limit_bytes=VMEM_LIMIT),
        name="diff_attention_t",
    )(lqk, subln_g, _attention_bias(tq, n_meta), qt, mk, mvt, kf, vt)


def _tail_kernel(x_ref, mc_ref, sga_ref, ya_ref, wb1_ref, wout_ref, g_ref, b_ref, wrt_ref, brt_ref,
                 h32_ref, hpk_ref, route_ref, counts_ref, *, alpha, n_experts, n_groups):
    @pl.when(pl.program_id(0) == 0)
    def _():
        counts_ref[...] = jnp.zeros_like(counts_ref)

    merged = mc_ref[...].astype(F32) + sga_ref[...].astype(F32) * jnp.dot(
        ya_ref[...], wb1_ref[...], preferred_element_type=F32)
    y = alpha * x_ref[...] + jnp.dot(merged.astype(BF16), wout_ref[...], preferred_element_type=F32)
    h = _layer_norm(y, g_ref[...], b_ref[...])
    h32_ref[...] = h
    hpk_ref[...] = _pack_bf16_pairs(h)

    logits = jnp.dot(h, wrt_ref[...], preferred_element_type=F32,
                     precision=lax.Precision.HIGHEST) + brt_ref[...]
    lane = lax.broadcasted_iota(jnp.int32, logits.shape, 1)
    big = jnp.int32(ROUTE_LANES)
    neg = -jnp.inf
    per_group = n_experts // n_groups

    def first_max(v):
        vmax = jnp.max(v, axis=-1, keepdims=True)
        return vmax, jnp.min(jnp.where(v == vmax, lane, big), axis=-1, keepdims=True)

    is_group = (lane >= n_experts) & (lane < n_experts + n_groups)
    lg = jnp.where(is_group, logits, neg)
    eg = jnp.exp(lg - jnp.max(lg, axis=-1, keepdims=True))
    p_group = jnp.where(is_group, eg / jnp.sum(eg, axis=-1, keepdims=True), neg)
    p_g, g_lane = first_max(p_group)
    g_idx = g_lane - n_experts
    lo = g_idx * per_group
    le = jnp.where((lane >= lo) & (lane < lo + per_group), logits, neg)
    v1, i1 = first_max(le)
    v2, i2 = first_max(jnp.where(lane == i1, neg, le))
    e2 = jnp.exp(v2 - v1)
    w1 = p_g / (1.0 + e2)
    w2 = p_g * e2 / (1.0 + e2)

    tm = logits.shape[0]
    onehot = jnp.where((lane == i1) | (lane == i2), 1.0, 0.0).astype(BF16)
    r = lax.broadcasted_iota(jnp.int32, (tm, tm), 0)
    c = lax.broadcasted_iota(jnp.int32, (tm, tm), 1)
    earlier = jnp.where(c < r, 1.0, 0.0).astype(BF16)
    before = jnp.dot(earlier, onehot, preferred_element_type=F32) + counts_ref[...]
    r1 = jnp.sum(jnp.where(lane == i1, before, 0.0), axis=-1, keepdims=True)
    r2 = jnp.sum(jnp.where(lane == i2, before, 0.0), axis=-1, keepdims=True)
    counts_ref[...] += jnp.sum(onehot.astype(F32), axis=0, keepdims=True)

    route = jnp.zeros_like(logits)
    for k, col in ((ROUTE_E0, i1.astype(F32)), (ROUTE_E1, i2.astype(F32)), (ROUTE_W0, w1), (ROUTE_W1, w2),
                   (ROUTE_R0, r1), (ROUTE_R1, r2)):
        route = jnp.where(lane == k, col, route)
    route_ref[...] = route


def _tail(x, mc, sga, ya, wb1, w_out, ln_g, ln_b, w_rt, b_rt, *, tm, alpha, n_experts, n_groups):
    t, d = x.shape
    assert t % tm == 0
    row = lambda w: pl.BlockSpec((tm, w), lambda i: (i, 0))
    return pl.pallas_call(
        functools.partial(_tail_kernel, alpha=alpha, n_experts=n_experts, n_groups=n_groups),
        grid=(t // tm,),
        in_specs=[row(d), row(d), row(d), row(d), _const_spec((d, d)), _const_spec((d, d)),
                  _const_spec((1, d)), _const_spec((1, d)),
                  _const_spec((d, ROUTE_LANES)), _const_spec((1, ROUTE_LANES))],
        out_specs=[row(d), row(d // 2), row(ROUTE_LANES), pl.BlockSpec((1, ROUTE_LANES), lambda i: (0, 0))],
        out_shape=[jax.ShapeDtypeStruct((t, d), F32), jax.ShapeDtypeStruct((t, d // 2), jnp.uint32),
                   jax.ShapeDtypeStruct((t, ROUTE_LANES), F32), jax.ShapeDtypeStruct((1, ROUTE_LANES), F32)],
        compiler_params=pltpu.CompilerParams(
            dimension_semantics=("arbitrary",), vmem_limit_bytes=VMEM_LIMIT),
        name="merge_ln_router",
    )(x, mc, sga, ya, wb1, w_out, ln_g, ln_b, w_rt, b_rt)


def _row_copy(src_ref, src_row, dst_ref, dst_row, sem):
    return pltpu.make_async_copy(src_ref.at[pl.ds(src_row, 1)], dst_ref.at[pl.ds(dst_row, 1)], sem)


def _moe_scatter_kernel(pos_ref, pad_rows_ref, n_pad_ref, n_used_ref, h_ref, xs_ref, zero_ref, sem, pad_sem,
                        *, tm):
    @pl.when(pl.program_id(0) == 0)
    def _():
        zero_ref[...] = jnp.zeros_like(zero_ref)
        tr = zero_ref.shape[0]
        n_tiles = xs_ref.shape[0] // tr

        def tile_copy(r):
            return pltpu.make_async_copy(zero_ref, xs_ref.at[pl.ds(pl.multiple_of(r * tr, tr), tr)], pad_sem)

        def issue_pad(j, _):
            _row_copy(zero_ref, 0, xs_ref, pad_rows_ref[j], pad_sem).start()
            return 0

        def wait_pad(j, _):
            _row_copy(zero_ref, 0, xs_ref, 0, pad_sem).wait()
            return 0

        lax.fori_loop(0, n_pad_ref[0], issue_pad, 0)
        lax.fori_loop(n_used_ref[0], n_tiles, lambda r, _: (tile_copy(r).start(), 0)[1], 0)
        lax.fori_loop(0, n_pad_ref[0], wait_pad, 0)
        lax.fori_loop(n_used_ref[0], n_tiles, lambda r, _: (tile_copy(r).wait(), 0)[1], 0)

    def issue(t, _):
        for k in range(TOP_K):
            _row_copy(h_ref, t, xs_ref, pos_ref[0, 0, k * tm + t], sem).start()
        return 0

    lax.fori_loop(0, tm, issue, 0)
    for _ in range(TOP_K):
        pltpu.make_async_copy(h_ref, xs_ref.at[pl.ds(0, tm)], sem).wait()


def _moe_scatter(pos, pad_rows, n_pad, n_used, hpk, n_rows, *, tm, tr):
    t, w = hpk.shape
    smem = pl.BlockSpec(memory_space=pltpu.SMEM)
    return pl.pallas_call(
        functools.partial(_moe_scatter_kernel, tm=tm),
        grid=(t // tm,),
        in_specs=[pl.BlockSpec((1, 1, TOP_K * tm), lambda i: (i, 0, 0), memory_space=pltpu.SMEM),
                  smem, smem, smem, pl.BlockSpec((tm, w), lambda i: (i, 0))],
        out_specs=pl.BlockSpec(memory_space=pl.ANY),
        out_shape=jax.ShapeDtypeStruct((n_rows, w), hpk.dtype),
        scratch_shapes=[pltpu.VMEM((tr, w), hpk.dtype), pltpu.SemaphoreType.DMA(()), pltpu.SemaphoreType.DMA(())],
        compiler_params=pltpu.CompilerParams(
            dimension_semantics=("arbitrary",), vmem_limit_bytes=VMEM_LIMIT, has_side_effects=True),
        name="moe_scatter",
    )(pos, pad_rows, n_pad, n_used, hpk)


def _moe_experts_kernel(tile_expert_ref, n_used_ref, xs_ref, wgu_ref, wd_ref, ys_ref, *, d_expert):
    del tile_expert_ref
    used = pl.program_id(0) < n_used_ref[0]

    @pl.when(used)
    def _():
        x_lo, x_hi = _unpack_bf16_pairs(xs_ref[...])
        half = x_lo.shape[1]
        ab = (jnp.dot(x_lo.astype(BF16), wgu_ref[0, :half, :], preferred_element_type=F32)
              + jnp.dot(x_hi.astype(BF16), wgu_ref[0, half:, :], preferred_element_type=F32))
        a = ab[:, :d_expert]
        act = (a * jax.nn.sigmoid(a) * ab[:, d_expert:]).astype(BF16)
        ys_ref[...] = _pack_bf16_pairs(jnp.dot(act, wd_ref[0], preferred_element_type=F32))

    @pl.when(jnp.logical_not(used))
    def _():
        ys_ref[...] = jnp.zeros_like(ys_ref)


def _moe_experts(tile_expert, n_used, xs, wgu, wd, *, tr):
    n_rows, w = xs.shape
    _, d, two_de = wgu.shape
    assert d == 2 * w
    last_used = lambda r, nu: jnp.minimum(r, nu[0] - 1)
    return pl.pallas_call(
        functools.partial(_moe_experts_kernel, d_expert=two_de // 2),
        grid_spec=pltpu.PrefetchScalarGridSpec(
            num_scalar_prefetch=2,
            grid=(n_rows // tr,),
            in_specs=[pl.BlockSpec((tr, w), lambda r, te, nu: (last_used(r, nu), 0)),
                      pl.BlockSpec((1, d, two_de), lambda r, te, nu: (te[r], 0, 0)),
                      pl.BlockSpec((1, two_de // 2, d), lambda r, te, nu: (te[r], 0, 0))],
            out_specs=pl.BlockSpec((tr, w), lambda r, te, nu: (r, 0))),
        out_shape=jax.ShapeDtypeStruct((n_rows, w), xs.dtype),
        compiler_params=pltpu.CompilerParams(
            dimension_semantics=("arbitrary",), vmem_limit_bytes=VMEM_LIMIT),
        name="moe_experts",
    )(tile_expert, n_used, xs, wgu, wd)


def _moe_combine_kernel(pos_ref, route_ref, h_ref, g_ref, b_ref, ys_ref, o_ref, y_buf, sem, *, tm, alpha):
    def issue(t, _):
        for k in range(TOP_K):
            _row_copy(ys_ref, pos_ref[0, 0, k * tm + t], y_buf.at[k], t, sem).start()
        return 0

    lax.fori_loop(0, tm, issue, 0)
    for k in range(TOP_K):
        pltpu.make_async_copy(ys_ref.at[pl.ds(0, tm)], y_buf.at[k], sem).wait()

    route = route_ref[...]
    lane = lax.broadcasted_iota(jnp.int32, route.shape, 1)
    f = jnp.zeros_like(h_ref)
    for k, w_lane in enumerate((ROUTE_W0, ROUTE_W1)):
        gate = jnp.sum(jnp.where(lane == w_lane, route, 0.0), axis=-1, keepdims=True)
        f = f + gate * jnp.concatenate(_unpack_bf16_pairs(y_buf[k]), axis=1)
    o_ref[...] = _layer_norm(alpha * h_ref[...] + f, g_ref[...], b_ref[...])


def _moe_combine(pos, route, h32, ys, ln_g, ln_b, *, tm, alpha):
    t, d = h32.shape
    return pl.pallas_call(
        functools.partial(_moe_combine_kernel, tm=tm, alpha=alpha),
        grid=(t // tm,),
        in_specs=[pl.BlockSpec((1, 1, TOP_K * tm), lambda i: (i, 0, 0), memory_space=pltpu.SMEM),
                  pl.BlockSpec((tm, ROUTE_LANES), lambda i: (i, 0)),
                  pl.BlockSpec((tm, d), lambda i: (i, 0)),
                  _const_spec((1, d)), _const_spec((1, d)),
                  pl.BlockSpec(memory_space=pl.ANY)],
        out_specs=pl.BlockSpec((tm, d), lambda i: (i, 0)),
        out_shape=jax.ShapeDtypeStruct((t, d), F32),
        scratch_shapes=[pltpu.VMEM((TOP_K, tm) + ys.shape[1:], ys.dtype), pltpu.SemaphoreType.DMA(())],
        compiler_params=pltpu.CompilerParams(
            dimension_semantics=("arbitrary",), vmem_limit_bytes=VMEM_LIMIT),
        name="moe_combine",
    )(pos, route, h32, ln_g, ln_b, ys)


def _moe(h32, hpk, route, counts, wgu, wd, ln_g, ln_b, *, tm, alpha):
    t, d = h32.shape
    n_experts = wgu.shape[0]
    tr = MOE_ROW_TILE
    assert t % tm == 0
    n_tiles = (TOP_K * t + n_experts * (tr - 1)) // tr

    def segment_of(i, seg_ends):
        return jnp.minimum(jnp.sum(i[:, None] >= seg_ends[None, :], axis=1), n_experts - 1).astype(jnp.int32)

    experts = jnp.arange(n_experts, dtype=jnp.int32)
    cnt = counts[0, :n_experts].astype(jnp.int32)
    padded = (cnt + tr - 1) // tr * tr
    ends = jnp.cumsum(padded)
    starts = ends - padded
    expert = route[:, ROUTE_E0:ROUTE_E1 + 1].astype(jnp.int32)
    rank = route[:, ROUTE_R0:ROUTE_R1 + 1].astype(jnp.int32)
    pos = jnp.sum(jnp.where(expert[..., None] == experts, starts, 0), axis=-1) + rank
    pos = pos.reshape(t // tm, tm, TOP_K).transpose(0, 2, 1).reshape(t // tm, 1, TOP_K * tm)
    tile_expert = segment_of(jnp.arange(n_tiles, dtype=jnp.int32) * tr, ends)
    n_used = ends[-1:] // tr
    pad_ends = jnp.cumsum(padded - cnt)
    i = jnp.arange(n_experts * (tr - 1), dtype=jnp.int32)
    seg = segment_of(i, pad_ends)
    pad_rows = jnp.minimum((starts + cnt)[seg] + i - (pad_ends - (padded - cnt))[seg], n_tiles * tr - 1)

    xs = _moe_scatter(pos, pad_rows, pad_ends[-1:], n_used, hpk, n_tiles * tr, tm=tm, tr=tr)
    ys = _moe_experts(tile_expert, n_used, xs, wgu, wd, tr=tr)
    return _moe_combine(pos, route, h32, ys, ln_g, ln_b, tm=tm, alpha=alpha)


def _pick_tile(n, target):
    t = min(n, target)
    while n % t:
        t //= 2
    return t


def kernel(x_prompt, x_sample, cache_k, cache_v, state_conv, meta_tokens, w_in, conv_w, lambda_qk, subln_g, w_branch, w_out, ln1_g, ln1_b, w_group, b_group, w_router, b_router, w_gate_up, w_down, ln2_g, ln2_b):
    depth = w_in.shape[0]
    assert depth == 1, "single-layer step only"
    bp, seq, d = x_prompt.shape
    bs, s_len, _ = x_sample.shape
    n_meta = meta_tokens.shape[0]
    n_heads = cache_k.shape[3]
    dqk = cache_k.shape[4] // 2
    past = cache_k.shape[2] - n_meta
    n_groups = w_group.shape[-1]
    n_experts = w_router.shape[-1]
    assert n_experts + n_groups <= ROUTE_LANES
    q_scale = dqk ** -0.5
    assert math.frexp(q_scale)[0] == 0.5, "the score scale is folded into q; exact only for powers of two"
    alpha = (2.0 * depth) ** 0.25
    lam_init = 0.8 - 0.6 * math.exp(-0.3 * 0)

    w_in_b = w_in[0].astype(BF16)
    wb = w_branch[0].astype(BF16)
    w_out_b = w_out[0].astype(BF16)
    wgu_b = w_gate_up[0].astype(BF16)
    wd_b = w_down[0].astype(BF16)
    w_rt = jnp.zeros((d, ROUTE_LANES), F32).at[:, :n_experts].set(w_router[0]).at[
        :, n_experts:n_experts + n_groups].set(w_group[0])
    b_rt = jnp.zeros((1, ROUTE_LANES), F32).at[0, :n_experts].set(b_router[0]).at[
        0, n_experts:n_experts + n_groups].set(b_group[0])

    def hist_rows(rows):
        return jnp.pad(rows, ((0, 0), (HIST - rows.shape[1], 0), (0, 0)))

    wqvt = jnp.concatenate([w_in_b[:, 3 * d:4 * d].T, w_in_b[:, 5 * d:6 * d].T], axis=0)
    proj = functools.partial(_proj_conv, w_in=w_in_b, conv_w=conv_w[0], wb0=wb[0], wqvt=wqvt, q_scale=q_scale)
    attn = functools.partial(_attention, lambda_qk[0], subln_g, n_heads=n_heads, lam_init=lam_init)
    tail = functools.partial(_tail, wb1=wb[1], w_out=w_out_b, ln_g=ln1_g, ln_b=ln1_b, w_rt=w_rt, b_rt=b_rt,
                             alpha=alpha, n_experts=n_experts, n_groups=n_groups)
    moe = functools.partial(_moe, wgu=wgu_b, wd=wd_b, ln_g=ln2_g, ln_b=ln2_b, alpha=alpha)

    mk32, mv32, _, mkb, mvb, _, _, mnc = proj(
        meta_tokens[None], jnp.zeros((1, HIST, d), F32), bb=1, ts=n_meta, transposed=False)
    ts = _pick_tile(seq, 256)
    mk_pad = jnp.pad(mkb[0], ((0, ts - n_meta), (0, 0)))
    mvt_pad = jnp.pad(mvb[0].T, ((0, 0), (0, ts - n_meta)))
    k32, v32, qt, kb, vt, mc, sga, nc = proj(
        x_prompt, jnp.broadcast_to(mnc, (bp, HIST, d)), bb=1, ts=ts, transposed=True)
    ya = _attention_t(lambda_qk[0], subln_g, qt, mk_pad, mvt_pad, kb, vt,
                      n_heads=n_heads, n_meta=n_meta, lam_init=lam_init)
    t_p = bp * seq
    tm = _pick_tile(t_p, 512)
    routed = tail(x_prompt.reshape(t_p, d), mc.reshape(t_p, d), sga.reshape(t_p, d), ya.reshape(t_p, d), tm=tm)
    y_prompt = moe(*routed, tm=tm).reshape(bp, seq, d)

    sk32, sv32, sqb, skb, svb, smc, ssga, snc = proj(
        x_sample, hist_rows(state_conv[0]), bb=bs, ts=s_len, transposed=False)
    sya = attn(sqb, cache_k[0].reshape(bs, n_meta + past, d), cache_v[0].reshape(bs, n_meta + past, d),
               skb, svb, tq=s_len, frame0=past)
    t_s = bs * s_len
    srouted = tail(x_sample.reshape(t_s, d), smc.reshape(t_s, d), ssga.reshape(t_s, d), sya.reshape(t_s, d),
                   tm=t_s)
    y_sample = moe(*srouted, tm=t_s).reshape(bs, s_len, d)

    def with_meta(m, f):
        full = jnp.concatenate([jnp.broadcast_to(m, (bp, n_meta, d)), f], axis=1)
        return full.reshape(1, bp, n_meta + seq, n_heads, d // n_heads)

    return (y_prompt, y_sample,
            with_meta(mk32, k32), with_meta(mv32, v32), nc[None, :, HIST - 2:],
            sk32.reshape(1, bs, s_len, n_heads, d // n_heads),
            sv32.reshape(1, bs, s_len, n_heads, d // n_heads), snc[None, :, HIST - 2:])
```

```python
import functools
import math

import jax
import jax.numpy as jnp
from jax import lax
from jax.experimental import pallas as pl
from jax.experimental.pallas import tpu as pltpu

CHUNK = 64
CHUNK_SHIFT = CHUNK.bit_length() - 1
assert 1 << CHUNK_SHIFT == CHUNK
LANES = 128
BF16_SUBLANES = 16
LOG2_E = math.log2(math.e)
BIAS_META, BIAS_DIAG = 0, 1
LN_EPS = 1e-5
RMS_EPS = 1e-5
HIST = 8
ROUTE_LANES = 128
ROUTE_E0, ROUTE_E1, ROUTE_W0, ROUTE_W1, ROUTE_R0, ROUTE_R1 = range(6)
TOP_K = 2
MOE_ROW_TILE = 512
VMEM_LIMIT = 52 * 1024 * 1024

F32 = jnp.float32
BF16 = jnp.bfloat16


def _const_spec(shape):
    return pl.BlockSpec(shape, lambda *_: (0,) * len(shape), pipeline_mode=pl.Buffered(1))


def _layer_norm(x, g, b):
    mu = jnp.mean(x, axis=-1, keepdims=True)
    xc = x - mu
    var = jnp.mean(xc * xc, axis=-1, keepdims=True)
    return xc * lax.rsqrt(var + LN_EPS) * g + b


def _proj_conv_kernel(x_ref, c0_ref, win_ref, cw_ref, wb0_ref, wqvt_ref,
                      k32_ref, v32_ref, q_ref, kb_ref, vb_ref, mc_ref, sga_ref, nc_ref,
                      carry_ref, *, bb, ts, d, q_scale, transposed):
    @pl.when(pl.program_id(1) == 0)
    def _():
        carry_ref[...] = c0_ref[...]

    xb = x_ref[...].reshape(bb * ts, d).astype(BF16)

    def proj(i):
        return jnp.dot(xb, win_ref[:, i * d:(i + 1) * d], preferred_element_type=F32)

    u = proj(2) * proj(0)
    cw = cw_ref[...]
    row = lax.broadcasted_iota(jnp.int32, (ts, d), 0)
    convs = []
    for b in range(bb):
        ub = u[b * ts:(b + 1) * ts]
        hist = carry_ref[b]
        h1 = hist[HIST - 1:HIST]
        h2 = hist[HIST - 2:HIST - 1]
        um1 = jnp.where(row == 0, h1, pltpu.roll(ub, 1, 0))
        um2 = jnp.where(row == 0, h2, jnp.where(row == 1, h1, pltpu.roll(ub, 2, 0)))
        convs.append(cw[0:1] * um2 + cw[1:2] * um1 + cw[2:3] * ub)
        carry_ref[b] = ub[ts - HIST:ts]
        nc_ref[b] = ub[ts - HIST:ts]
    conv = convs[0] if bb == 1 else jnp.concatenate(convs, axis=0)
    yc = (proj(1) * conv).astype(BF16)
    mc = jax.nn.sigmoid(proj(6)) * jnp.dot(yc, wb0_ref[...], preferred_element_type=F32)
    mc_ref[...] = mc.astype(BF16).reshape(bb, ts, d)
    sga_ref[...] = jax.nn.sigmoid(proj(7)).astype(BF16).reshape(bb, ts, d)
    k = proj(4)
    k32_ref[...] = k.reshape(bb, ts, d)
    kb_ref[...] = k.astype(BF16).reshape(bb, ts, d)
    v = proj(5)
    v32_ref[...] = v.reshape(bb, ts, d)
    if transposed:
        nt = (((1,), (1,)), ((), ()))
        qt = lax.dot_general(wqvt_ref[0:d, :], xb, nt, preferred_element_type=F32)
        q_ref[0, 0] = (qt * (q_scale * LOG2_E)).astype(BF16)
        vb_ref[0, 0] = lax.dot_general(wqvt_ref[d:2 * d, :], xb, nt, preferred_element_type=F32).astype(BF16)
    else:
        q_ref[...] = (proj(3) * q_scale).astype(BF16).reshape(bb, ts, d)
        vb_ref[...] = v.astype(BF16).reshape(bb, ts, d)


def _proj_conv(x, c0, w_in, conv_w, wb0, wqvt, *, bb, ts, q_scale, transposed):
    nb, length, d = x.shape
    assert nb % bb == 0 and length % ts == 0 and ts % HIST == 0
    assert w_in.shape == (d, 8 * d), "all eight projection sections must be d_model wide"
    assert not transposed or bb == 1
    blk = pl.BlockSpec((bb, ts, d), lambda b, s: (b, s, 0))
    hist_blk = pl.BlockSpec((bb, HIST, d), lambda b, s: (b, 0, 0))
    f32_out = jax.ShapeDtypeStruct((nb, length, d), F32)
    bf_out = jax.ShapeDtypeStruct((nb, length, d), BF16)
    if transposed:
        qv_blk = pl.BlockSpec((1, 1, d, ts), lambda b, s: (b, s, 0, 0))
        qv_out = jax.ShapeDtypeStruct((nb, length // ts, d, ts), BF16)
    else:
        qv_blk, qv_out = blk, bf_out
    return pl.pallas_call(
        functools.partial(_proj_conv_kernel, bb=bb, ts=ts, d=d, q_scale=q_scale, transposed=transposed),
        grid=(nb // bb, length // ts),
        in_specs=[blk, hist_blk, _const_spec((d, 8 * d)), _const_spec((3, d)), _const_spec((d, d)),
                  _const_spec((2 * d, d))],
        out_specs=[blk, blk, qv_blk, blk, qv_blk, blk, blk, hist_blk],
        out_shape=[f32_out, f32_out, qv_out, bf_out, qv_out, bf_out, bf_out,
                   jax.ShapeDtypeStruct((nb, HIST, d), F32)],
        scratch_shapes=[pltpu.VMEM((bb, HIST, d), F32)],
        compiler_params=pltpu.CompilerParams(
            dimension_semantics=("parallel", "arbitrary"), vmem_limit_bytes=VMEM_LIMIT),
        name="proj_conv",
    )(x, c0, w_in, conv_w, wb0, wqvt)


def _pad_keys(x):
    pad = -x.shape[0] % LANES
    return x if pad == 0 else jnp.concatenate([x, jnp.zeros((pad, x.shape[1]), x.dtype)], axis=0)


def _attn_kernel(lqk_ref, g_ref, q_ref, ck_ref, cv_ref, kf_ref, vf_ref, o_ref,
                 *, tq, frame0, lam_init, dqk):
    qi = pl.program_id(2)
    lqk = lqk_ref[...]
    lam = (jnp.exp(jnp.sum(lqk[0:1] * lqk[1:2], axis=-1, keepdims=True))
           - jnp.exp(jnp.sum(lqk[2:3] * lqk[3:4], axis=-1, keepdims=True)) + lam_init)

    q = q_ref[0]
    lane = lax.broadcasted_iota(jnp.int32, q.shape, 1)
    zero = jnp.zeros_like(q)
    qs = jnp.concatenate([jnp.where(lane < dqk, q, zero), jnp.where(lane >= dqk, q, zero)], axis=0)

    def scores(k):
        return lax.dot_general(qs, k, (((1,), (1,)), ((), ())), preferred_element_type=F32)

    n_ctx = ck_ref.shape[1]
    s = scores(_pad_keys(ck_ref[0].astype(BF16)))
    if n_ctx % LANES:
        s = jnp.where(lax.broadcasted_iota(jnp.int32, s.shape, 1) < n_ctx, s, -jnp.inf)
    m = jnp.max(s, axis=-1, keepdims=True)
    p = jnp.exp(s - m)
    l = jnp.sum(p, axis=-1, keepdims=True)
    acc = jnp.dot(p.astype(BF16), _pad_keys(cv_ref[0].astype(BF16)), preferred_element_type=F32)

    def step(j, carry, masked):
        m, l, acc = carry
        start = pl.multiple_of(j * tq, tq)
        s = scores(_pad_keys(kf_ref[0, pl.ds(start, tq), :]))
        if masked:
            r = lax.broadcasted_iota(jnp.int32, s.shape, 0)
            c = lax.broadcasted_iota(jnp.int32, s.shape, 1)
            q_chunk = lax.shift_right_logical(frame0 + j * tq + jnp.where(r >= tq, r - tq, r), CHUNK_SHIFT)
            k_chunk = lax.shift_right_logical(frame0 + j * tq + c, CHUNK_SHIFT)
            s = jnp.where((k_chunk <= q_chunk) & (c < tq), s, -jnp.inf)
        m_new = jnp.maximum(m, jnp.max(s, axis=-1, keepdims=True))
        alpha = jnp.exp(m - m_new)
        p = jnp.exp(s - m_new)
        l = alpha * l + jnp.sum(p, axis=-1, keepdims=True)
        acc = alpha * acc + jnp.dot(p.astype(BF16), _pad_keys(vf_ref[0, pl.ds(start, tq), :]),
                                    preferred_element_type=F32)
        return m_new, l, acc

    carry = lax.fori_loop(0, qi, functools.partial(step, masked=False), (m, l, acc))
    m, l, acc = step(qi, carry, masked=True)

    o = acc[:tq] / l[:tq] - lam * (acc[tq:] / l[tq:])
    o = o * lax.rsqrt(jnp.mean(o * o, axis=-1, keepdims=True) + RMS_EPS) * g_ref[...] * (1.0 - lam_init)
    o_ref[0] = o.astype(o_ref.dtype)


def _attention(lqk, subln_g, q, ctx_k, ctx_v, kf, vf, *, n_heads, tq, frame0, lam_init):
    nb, length, d = q.shape
    dv = d // n_heads
    dqk = lqk.shape[1]
    assert dv == 2 * dqk and length % tq == 0
    assert tq % LANES == 0 or length == tq, "only a single (diagonal) frame tile may be lane-padded"
    n_ctx = ctx_k.shape[1]
    ctx_map = (lambda b, h, i: (b, 0, h)) if ctx_k.shape[0] == nb else (lambda b, h, i: (0, 0, h))
    q_blk = pl.BlockSpec((1, tq, dv), lambda b, h, i: (b, i, h))
    ctx_blk = pl.BlockSpec((1, n_ctx, dv), ctx_map)
    kv_blk = pl.BlockSpec((1, length, dv), lambda b, h, i: (b, 0, h))
    return pl.pallas_call(
        functools.partial(_attn_kernel, tq=tq, frame0=frame0, lam_init=lam_init, dqk=dqk),
        grid=(nb, n_heads, length // tq),
        in_specs=[pl.BlockSpec((4, dqk), lambda b, h, i: (0, 0)),
                  pl.BlockSpec((1, dv), lambda b, h, i: (0, 0)),
                  q_blk, ctx_blk, ctx_blk, kv_blk, kv_blk],
        out_specs=q_blk,
        out_shape=jax.ShapeDtypeStruct((nb, length, d), BF16),
        compiler_params=pltpu.CompilerParams(
            dimension_semantics=("parallel", "parallel", "arbitrary"), vmem_limit_bytes=VMEM_LIMIT),
        name="diff_attention",
    )(lqk, subln_g, q, ctx_k, ctx_v, kf, vf)


def _attn_t_kernel(lqk_ref, g_ref, bias_ref, qt_ref, mk_ref, mvt_ref, kf_ref, vt_ref, o_ref,
                   *, tq, n_tiles, lam_init, dqk):
    lqk = lqk_ref[...]
    lam = (jnp.exp(jnp.sum(lqk[0:1] * lqk[1:2], axis=-1, keepdims=True))
           - jnp.exp(jnp.sum(lqk[2:3] * lqk[3:4], axis=-1, keepdims=True)) + lam_init)

    dv = qt_ref.shape[2]
    row = lax.broadcasted_iota(jnp.int32, (dv, tq), 0)
    zero = jnp.zeros((dv, tq), BF16)
    ones = jnp.ones((BF16_SUBLANES, tq), BF16)

    def query_tile(q):
        n = q + 2
        qt = qt_ref[0, q]
        w = jnp.concatenate([jnp.where(row < dqk, qt, zero), jnp.where(row >= dqk, qt, zero)], axis=1)

        def scores(t):
            if t == 0:
                return jnp.dot(mk_ref[...], w, preferred_element_type=F32) + bias_ref[BIAS_META]
            s = jnp.dot(kf_ref[0, (t - 1) * tq:t * tq, :], w, preferred_element_type=F32)
            return s + bias_ref[BIAS_DIAG] if t == n - 1 else s

        s = scores(0)
        for t in range(n):
            s_next = scores(t + 1) if t + 1 < n else None
            vt = jnp.concatenate([mvt_ref[...] if t == 0 else vt_ref[0, t - 1], ones], axis=0)
            s_max = jnp.max(s, axis=0, keepdims=True)
            if t == 0:
                m = s_max
                acc = jnp.dot(vt, jnp.exp2(s - m).astype(BF16), preferred_element_type=F32)
            else:
                m_new = jnp.maximum(m, s_max)
                acc = jnp.exp2(m - m_new) * acc + jnp.dot(vt, jnp.exp2(s - m_new).astype(BF16),
                                                          preferred_element_type=F32)
                m = m_new
            s = s_next

        l = acc[dv:dv + 1]
        ot = acc[:dv, :tq] / l[:, :tq] - lam * (acc[:dv, tq:] / l[:, tq:])
        ot = ot * lax.rsqrt(jnp.mean(ot * ot, axis=0, keepdims=True) + RMS_EPS)
        o_ref[0, q * tq:(q + 1) * tq, :] = (ot.T * g_ref[...] * (1.0 - lam_init)).astype(o_ref.dtype)

    for q in range(n_tiles):
        query_tile(q)


def _attention_bias(tq, n_meta):
    r = jnp.arange(tq, dtype=jnp.int32)[:, None]
    c = jnp.arange(2 * tq, dtype=jnp.int32)[None, :] % tq
    meta = jnp.broadcast_to(r < n_meta, (tq, 2 * tq))
    diag = (r // CHUNK) <= (c // CHUNK)
    neg = jnp.float32(-jnp.inf)
    return jnp.stack([jnp.where(meta, 0.0, neg), jnp.where(diag, 0.0, neg)])


def _attention_t(lqk, subln_g, qt, mk, mvt, kf, vt, *, n_heads, n_meta, lam_init):
    nb, n_tiles, d, tq = qt.shape
    length = n_tiles * tq
    dv = d // n_heads
    dqk = lqk.shape[1]
    assert dv == 2 * dqk and kf.shape == (nb, length, d) and vt.shape == qt.shape
    assert mk.shape == (tq, d) and mvt.shape == (d, tq) and n_meta <= tq and tq % CHUNK == 0
    return pl.pallas_call(
        functools.partial(_attn_t_kernel, tq=tq, n_tiles=n_tiles, lam_init=lam_init, dqk=dqk),
        grid=(nb, n_heads),
        in_specs=[pl.BlockSpec((4, dqk), lambda b, h: (0, 0)),
                  pl.BlockSpec((1, dv), lambda b, h: (0, 0)),
                  _const_spec((2, tq, 2 * tq)),
                  pl.BlockSpec((1, n_tiles, dv, tq), lambda b, h: (b, 0, h, 0)),
                  pl.BlockSpec((tq, dv), lambda b, h: (0, h)),
                  pl.BlockSpec((dv, tq), lambda b, h: (h, 0)),
                  pl.BlockSpec((1, length, dv), lambda b, h: (b, 0, h)),
                  pl.BlockSpec((1, n_tiles, dv, tq), lambda b, h: (b, 0, h, 0))],
        out_specs=pl.BlockSpec((1, length, dv), lambda b, h: (b, 0, h)),
        out_shape=jax.ShapeDtypeStruct((nb, length, d), BF16),
        compiler_params=pltpu.CompilerParams(
            dimension_semantics=("parallel", "parallel"), vmem_limit_bytes=VMEM_LIMIT),
        name="diff_attention_t",
    )(lqk, subln_g, _attention_bias(tq, n_meta), qt, mk, mvt, kf, vt)


def _tail_kernel(x_ref, mc_ref, sga_ref, ya_ref, wb1_ref, wout_ref, g_ref, b_ref, wrt_ref, brt_ref,
                 h32_ref, route_ref, counts_ref, *, alpha, n_experts, n_groups):
    @pl.when(pl.program_id(0) == 0)
    def _():
        counts_ref[...] = jnp.zeros_like(counts_ref)

    merged = mc_ref[...].astype(F32) + sga_ref[...].astype(F32) * jnp.dot(
        ya_ref[...], wb1_ref[...], preferred_element_type=F32)
    y = alpha * x_ref[...] + jnp.dot(merged.astype(BF16), wout_ref[...], preferred_element_type=F32)
    h = _layer_norm(y, g_ref[...], b_ref[...])
    h32_ref[...] = h

    logits = jnp.dot(h, wrt_ref[...], preferred_element_type=F32,
                     precision=lax.Precision.HIGHEST) + brt_ref[...]
    lane = lax.broadcasted_iota(jnp.int32, logits.shape, 1)
    big = jnp.int32(ROUTE_LANES)
    neg = -jnp.inf
    per_group = n_experts // n_groups

    def first_max(v):
        vmax = jnp.max(v, axis=-1, keepdims=True)
        return vmax, jnp.min(jnp.where(v == vmax, lane, big), axis=-1, keepdims=True)

    is_group = (lane >= n_experts) & (lane < n_experts + n_groups)
    lg = jnp.where(is_group, logits, neg)
    eg = jnp.exp(lg - jnp.max(lg, axis=-1, keepdims=True))
    p_group = jnp.where(is_group, eg / jnp.sum(eg, axis=-1, keepdims=True), neg)
    p_g, g_lane = first_max(p_group)
    g_idx = g_lane - n_experts
    lo = g_idx * per_group
    le = jnp.where((lane >= lo) & (lane < lo + per_group), logits, neg)
    v1, i1 = first_max(le)
    v2, i2 = first_max(jnp.where(lane == i1, neg, le))
    e2 = jnp.exp(v2 - v1)
    w1 = p_g / (1.0 + e2)
    w2 = p_g * e2 / (1.0 + e2)

    tm = logits.shape[0]
    onehot = jnp.where((lane == i1) | (lane == i2), 1.0, 0.0).astype(BF16)
    r = lax.broadcasted_iota(jnp.int32, (tm, tm), 0)
    c = lax.broadcasted_iota(jnp.int32, (tm, tm), 1)
    earlier = jnp.where(c < r, 1.0, 0.0).astype(BF16)
    before = jnp.dot(earlier, onehot, preferred_element_type=F32) + counts_ref[...]
    r1 = jnp.sum(jnp.where(lane == i1, before, 0.0), axis=-1, keepdims=True)
    r2 = jnp.sum(jnp.where(lane == i2, before, 0.0), axis=-1, keepdims=True)
    counts_ref[...] += jnp.sum(onehot.astype(F32), axis=0, keepdims=True)

    route = jnp.zeros_like(logits)
    for k, col in ((ROUTE_E0, i1.astype(F32)), (ROUTE_E1, i2.astype(F32)), (ROUTE_W0, w1), (ROUTE_W1, w2),
                   (ROUTE_R0, r1), (ROUTE_R1, r2)):
        route = jnp.where(lane == k, col, route)
    route_ref[...] = route


def _tail(x, mc, sga, ya, wb1, w_out, ln_g, ln_b, w_rt, b_rt, *, tm, alpha, n_experts, n_groups):
    t, d = x.shape
    assert t % tm == 0
    row = lambda w: pl.BlockSpec((tm, w), lambda i: (i, 0))
    return pl.pallas_call(
        functools.partial(_tail_kernel, alpha=alpha, n_experts=n_experts, n_groups=n_groups),
        grid=(t // tm,),
        in_specs=[row(d), row(d), row(d), row(d), _const_spec((d, d)), _const_spec((d, d)),
                  _const_spec((1, d)), _const_spec((1, d)),
                  _const_spec((d, ROUTE_LANES)), _const_spec((1, ROUTE_LANES))],
        out_specs=[row(d), row(ROUTE_LANES), pl.BlockSpec((1, ROUTE_LANES), lambda i: (0, 0))],
        out_shape=[jax.ShapeDtypeStruct((t, d), F32), jax.ShapeDtypeStruct((t, ROUTE_LANES), F32),
                   jax.ShapeDtypeStruct((1, ROUTE_LANES), F32)],
        compiler_params=pltpu.CompilerParams(
            dimension_semantics=("arbitrary",), vmem_limit_bytes=VMEM_LIMIT),
        name="merge_ln_router",
    )(x, mc, sga, ya, wb1, w_out, ln_g, ln_b, w_rt, b_rt)


def _row_copy(src_ref, src_row, dst_ref, dst_row, sem):
    return pltpu.make_async_copy(src_ref.at[pl.ds(src_row, 1)], dst_ref.at[pl.ds(dst_row, 1)], sem)


def _moe_scatter_kernel(pos_ref, pad_rows_ref, n_pad_ref, n_used_ref, h_ref, xs_ref, zero_ref, sem, pad_sem,
                        *, tm):
    @pl.when(pl.program_id(0) == 0)
    def _():
        zero_ref[...] = jnp.zeros_like(zero_ref)
        tr = zero_ref.shape[0]
        n_tiles = xs_ref.shape[0] // tr

        def tile_copy(r):
            return pltpu.make_async_copy(zero_ref, xs_ref.at[pl.ds(pl.multiple_of(r * tr, tr), tr)], pad_sem)

        def issue_pad(j, _):
            _row_copy(zero_ref, 0, xs_ref, pad_rows_ref[j], pad_sem).start()
            return 0

        def wait_pad(j, _):
            _row_copy(zero_ref, 0, xs_ref, 0, pad_sem).wait()
            return 0

        lax.fori_loop(0, n_pad_ref[0], issue_pad, 0)
        lax.fori_loop(n_used_ref[0], n_tiles, lambda r, _: (tile_copy(r).start(), 0)[1], 0)
        lax.fori_loop(0, n_pad_ref[0], wait_pad, 0)
        lax.fori_loop(n_used_ref[0], n_tiles, lambda r, _: (tile_copy(r).wait(), 0)[1], 0)

    def issue(t, _):
        for k in range(TOP_K):
            _row_copy(h_ref, t, xs_ref, pos_ref[0, 0, k * tm + t], sem).start()
        return 0

    lax.fori_loop(0, tm, issue, 0)
    for _ in range(TOP_K):
        pltpu.make_async_copy(h_ref, xs_ref.at[pl.ds(0, tm)], sem).wait()


def _moe_scatter(pos, pad_rows, n_pad, n_used, h32, n_rows, *, tm, tr):
    t, w = h32.shape
    smem = pl.BlockSpec(memory_space=pltpu.SMEM)
    return pl.pallas_call(
        functools.partial(_moe_scatter_kernel, tm=tm),
        grid=(t // tm,),
        in_specs=[pl.BlockSpec((1, 1, TOP_K * tm), lambda i: (i, 0, 0), memory_space=pltpu.SMEM),
                  smem, smem, smem, pl.BlockSpec((tm, w), lambda i: (i, 0))],
        out_specs=pl.BlockSpec(memory_space=pl.ANY),
        out_shape=jax.ShapeDtypeStruct((n_rows, w), h32.dtype),
        scratch_shapes=[pltpu.VMEM((tr, w), h32.dtype), pltpu.SemaphoreType.DMA(()), pltpu.SemaphoreType.DMA(())],
        compiler_params=pltpu.CompilerParams(
            dimension_semantics=("arbitrary",), vmem_limit_bytes=VMEM_LIMIT, has_side_effects=True),
        name="moe_scatter",
    )(pos, pad_rows, n_pad, n_used, h32)


def _moe_experts_kernel(tile_expert_ref, n_used_ref, xs_ref, wgu_ref, wd_ref, ys_ref, *, d_expert):
    del tile_expert_ref
    used = pl.program_id(0) < n_used_ref[0]

    @pl.when(used)
    def _():
        ab = jnp.dot(xs_ref[...].astype(BF16), wgu_ref[0], preferred_element_type=F32)
        a = ab[:, :d_expert]
        act = (a * jax.nn.sigmoid(a) * ab[:, d_expert:]).astype(BF16)
        ys_ref[...] = jnp.dot(act, wd_ref[0], preferred_element_type=F32)

    @pl.when(jnp.logical_not(used))
    def _():
        ys_ref[...] = jnp.zeros_like(ys_ref)


def _moe_experts(tile_expert, n_used, xs, wgu, wd, *, tr):
    n_rows, w = xs.shape
    _, d, two_de = wgu.shape
    assert d == w
    last_used = lambda r, nu: jnp.minimum(r, nu[0] - 1)
    return pl.pallas_call(
        functools.partial(_moe_experts_kernel, d_expert=two_de // 2),
        grid_spec=pltpu.PrefetchScalarGridSpec(
            num_scalar_prefetch=2,
            grid=(n_rows // tr,),
            in_specs=[pl.BlockSpec((tr, w), lambda r, te, nu: (last_used(r, nu), 0)),
                      pl.BlockSpec((1, d, two_de), lambda r, te, nu: (te[r], 0, 0)),
                      pl.BlockSpec((1, two_de // 2, d), lambda r, te, nu: (te[r], 0, 0))],
            out_specs=pl.BlockSpec((tr, w), lambda r, te, nu: (r, 0))),
        out_shape=jax.ShapeDtypeStruct((n_rows, w), xs.dtype),
        compiler_params=pltpu.CompilerParams(
            dimension_semantics=("arbitrary",), vmem_limit_bytes=VMEM_LIMIT),
        name="moe_experts",
    )(tile_expert, n_used, xs, wgu, wd)


def _moe_combine_kernel(pos_ref, route_ref, h_ref, g_ref, b_ref, ys_ref, o_ref, y_buf, sem, *, tm, alpha):
    def issue(t, _):
        for k in range(TOP_K):
            _row_copy(ys_ref, pos_ref[0, 0, k * tm + t], y_buf.at[k], t, sem).start()
        return 0

    lax.fori_loop(0, tm, issue, 0)
    for k in range(TOP_K):
        pltpu.make_async_copy(ys_ref.at[pl.ds(0, tm)], y_buf.at[k], sem).wait()

    route = route_ref[...]
    lane = lax.broadcasted_iota(jnp.int32, route.shape, 1)
    f = jnp.zeros_like(h_ref)
    for k, w_lane in enumerate((ROUTE_W0, ROUTE_W1)):
        gate = jnp.sum(jnp.where(lane == w_lane, route, 0.0), axis=-1, keepdims=True)
        f = f + gate * y_buf[k]
    o_ref[...] = _layer_norm(alpha * h_ref[...] + f, g_ref[...], b_ref[...])


def _moe_combine(pos, route, h32, ys, ln_g, ln_b, *, tm, alpha):
    t, d = h32.shape
    return pl.pallas_call(
        functools.partial(_moe_combine_kernel, tm=tm, alpha=alpha),
        grid=(t // tm,),
        in_specs=[pl.BlockSpec((1, 1, TOP_K * tm), lambda i: (i, 0, 0), memory_space=pltpu.SMEM),
                  pl.BlockSpec((tm, ROUTE_LANES), lambda i: (i, 0)),
                  pl.BlockSpec((tm, d), lambda i: (i, 0)),
                  _const_spec((1, d)), _const_spec((1, d)),
                  pl.BlockSpec(memory_space=pl.ANY)],
        out_specs=pl.BlockSpec((tm, d), lambda i: (i, 0)),
        out_shape=jax.ShapeDtypeStruct((t, d), F32),
        scratch_shapes=[pltpu.VMEM((TOP_K, tm) + ys.shape[1:], ys.dtype), pltpu.SemaphoreType.DMA(())],
        compiler_params=pltpu.CompilerParams(
            dimension_semantics=("arbitrary",), vmem_limit_bytes=VMEM_LIMIT),
        name="moe_combine",
    )(pos, route, h32, ln_g, ln_b, ys)


def _moe(h32, route, counts, wgu, wd, ln_g, ln_b, *, tm, alpha):
    t, d = h32.shape
    n_experts = wgu.shape[0]
    tr = min(MOE_ROW_TILE, max(LANES, TOP_K * t // n_experts))
    assert t % tm == 0
    n_tiles = (TOP_K * t + n_experts * (tr - 1)) // tr

    def segment_of(i, seg_ends):
        return jnp.minimum(jnp.sum(i[:, None] >= seg_ends[None, :], axis=1), n_experts - 1).astype(jnp.int32)

    experts = jnp.arange(n_experts, dtype=jnp.int32)
    cnt = counts[0, :n_experts].astype(jnp.int32)
    padded = (cnt + tr - 1) // tr * tr
    ends = jnp.cumsum(padded)
    starts = ends - padded
    expert = route[:, ROUTE_E0:ROUTE_E1 + 1].astype(jnp.int32)
    rank = route[:, ROUTE_R0:ROUTE_R1 + 1].astype(jnp.int32)
    pos = jnp.sum(jnp.where(expert[..., None] == experts, starts, 0), axis=-1) + rank
    pos = pos.reshape(t // tm, tm, TOP_K).transpose(0, 2, 1).reshape(t // tm, 1, TOP_K * tm)
    tile_expert = segment_of(jnp.arange(n_tiles, dtype=jnp.int32) * tr, ends)
    n_used = ends[-1:] // tr
    pad_ends = jnp.cumsum(padded - cnt)
    i = jnp.arange(n_experts * (tr - 1), dtype=jnp.int32)
    seg = segment_of(i, pad_ends)
    pad_rows = jnp.minimum((starts + cnt)[seg] + i - (pad_ends - (padded - cnt))[seg], n_tiles * tr - 1)

    xs = _moe_scatter(pos, pad_rows, pad_ends[-1:], n_used, h32, n_tiles * tr, tm=tm, tr=tr)
    ys = _moe_experts(tile_expert, n_used, xs, wgu, wd, tr=tr)
    return _moe_combine(pos, route, h32, ys, ln_g, ln_b, tm=tm, alpha=alpha)


def _pick_tile(n, target):
    t = min(n, target)
    while n % t:
        t //= 2
    return t


def kernel(x_prompt, x_sample, cache_k, cache_v, state_conv, meta_tokens, w_in, conv_w, lambda_qk, subln_g, w_branch, w_out, ln1_g, ln1_b, w_group, b_group, w_router, b_router, w_gate_up, w_down, ln2_g, ln2_b):
    depth = w_in.shape[0]
    assert depth == 1, "single-layer step only"
    bp, seq, d = x_prompt.shape
    bs, s_len, _ = x_sample.shape
    n_meta = meta_tokens.shape[0]
    n_heads = cache_k.shape[3]
    dqk = cache_k.shape[4] // 2
    past = cache_k.shape[2] - n_meta
    n_groups = w_group.shape[-1]
    n_experts = w_router.shape[-1]
    assert n_experts + n_groups <= ROUTE_LANES
    q_scale = dqk ** -0.5
    assert math.frexp(q_scale)[0] == 0.5, "the score scale is folded into q; exact only for powers of two"
    alpha = (2.0 * depth) ** 0.25
    lam_init = 0.8 - 0.6 * math.exp(-0.3 * 0)

    w_in_b = w_in[0].astype(BF16)
    wb = w_branch[0].astype(BF16)
    w_out_b = w_out[0].astype(BF16)
    wgu_b = w_gate_up[0].astype(BF16)
    wd_b = w_down[0].astype(BF16)
    w_rt = jnp.zeros((d, ROUTE_LANES), F32).at[:, :n_experts].set(w_router[0]).at[
        :, n_experts:n_experts + n_groups].set(w_group[0])
    b_rt = jnp.zeros((1, ROUTE_LANES), F32).at[0, :n_experts].set(b_router[0]).at[
        0, n_experts:n_experts + n_groups].set(b_group[0])

    def hist_rows(rows):
        return jnp.pad(rows, ((0, 0), (HIST - rows.shape[1], 0), (0, 0)))

    wqvt = jnp.concatenate([w_in_b[:, 3 * d:4 * d].T, w_in_b[:, 5 * d:6 * d].T], axis=0)
    proj = functools.partial(_proj_conv, w_in=w_in_b, conv_w=conv_w[0], wb0=wb[0], wqvt=wqvt, q_scale=q_scale)
    attn = functools.partial(_attention, lambda_qk[0], subln_g, n_heads=n_heads, lam_init=lam_init)
    tail = functools.partial(_tail, wb1=wb[1], w_out=w_out_b, ln_g=ln1_g, ln_b=ln1_b, w_rt=w_rt, b_rt=b_rt,
                             alpha=alpha, n_experts=n_experts, n_groups=n_groups)
    moe = functools.partial(_moe, wgu=wgu_b, wd=wd_b, ln_g=ln2_g, ln_b=ln2_b, alpha=alpha)

    mk32, mv32, _, mkb, mvb, _, _, mnc = proj(
        meta_tokens[None], jnp.zeros((1, HIST, d), F32), bb=1, ts=n_meta, transposed=False)
    ts = _pick_tile(seq, 256)
    mk_pad = jnp.pad(mkb[0], ((0, ts - n_meta), (0, 0)))
    mvt_pad = jnp.pad(mvb[0].T, ((0, 0), (0, ts - n_meta)))
    k32, v32, qt, kb, vt, mc, sga, nc = proj(
        x_prompt, jnp.broadcast_to(mnc, (bp, HIST, d)), bb=1, ts=ts, transposed=True)
    ya = _attention_t(lambda_qk[0], subln_g, qt, mk_pad, mvt_pad, kb, vt,
                      n_heads=n_heads, n_meta=n_meta, lam_init=lam_init)
    t_p = bp * seq
    tm = _pick_tile(t_p, 512)
    routed = tail(x_prompt.reshape(t_p, d), mc.reshape(t_p, d), sga.reshape(t_p, d), ya.reshape(t_p, d), tm=tm)
    y_prompt = moe(*routed, tm=tm).reshape(bp, seq, d)

    sk32, sv32, sqb, skb, svb, smc, ssga, snc = proj(
        x_sample, hist_rows(state_conv[0]), bb=bs, ts=s_len, transposed=False)
    sya = attn(sqb, cache_k[0].reshape(bs, n_meta + past, d), cache_v[0].reshape(bs, n_meta + past, d),
               skb, svb, tq=s_len, frame0=past)
    t_s = bs * s_len
    srouted = tail(x_sample.reshape(t_s, d), smc.reshape(t_s, d), ssga.reshape(t_s, d), sya.reshape(t_s, d),
                   tm=t_s)
    y_sample = moe(*srouted, tm=t_s).reshape(bs, s_len, d)

    def with_meta(m, f):
        full = jnp.concatenate([jnp.broadcast_to(m, (bp, n_meta, d)), f], axis=1)
        return full.reshape(1, bp, n_meta + seq, n_heads, d // n_heads)

    return (y_prompt, y_sample,
            with_meta(mk32, k32), with_meta(mv32, v32), nc[None, :, HIST - 2:],
            sk32.reshape(1, bs, s_len, n_heads, d // n_heads),
            sv32.reshape(1, bs, s_len, n_heads, d // n_heads), snc[None, :, HIST - 2:])
```

```python
import functools
import math

import jax
import jax.numpy as jnp
from jax import lax
from jax.experimental import pallas as pl
from jax.experimental.pallas import tpu as pltpu

CHUNK = 64
CHUNK_SHIFT = CHUNK.bit_length() - 1
assert 1 << CHUNK_SHIFT == CHUNK
LANES = 128
BF16_SUBLANES = 16
LOG2_E = math.log2(math.e)
BIAS_META, BIAS_DIAG = 0, 1
LN_EPS = 1e-5
RMS_EPS = 1e-5
HIST = 8
ROUTE_LANES = 128
ROUTE_E0, ROUTE_E1, ROUTE_W0, ROUTE_W1, ROUTE_R0, ROUTE_R1 = range(6)
TOP_K = 2
MOE_ROW_TILE = 512
PROJ_TILE = 512
ATTN_TILE = 256
VMEM_LIMIT = 52 * 1024 * 1024

F32 = jnp.float32
BF16 = jnp.bfloat16


def _const_spec(shape):
    return pl.BlockSpec(shape, lambda *_: (0,) * len(shape), pipeline_mode=pl.Buffered(1))


def _layer_norm(x, g, b):
    mu = jnp.mean(x, axis=-1, keepdims=True)
    xc = x - mu
    var = jnp.mean(xc * xc, axis=-1, keepdims=True)
    return xc * lax.rsqrt(var + LN_EPS) * g + b


def _proj_conv_kernel(x_ref, c0_ref, win_ref, cw_ref, wb0_ref, wqvt_ref,
                      k32_ref, v32_ref, q_ref, kb_ref, vb_ref, mc_ref, sga_ref, nc_ref,
                      carry_ref, *, bb, ts, d, q_scale, transposed):
    @pl.when(pl.program_id(1) == 0)
    def _():
        carry_ref[...] = c0_ref[...]

    xb = x_ref[...].reshape(bb * ts, d).astype(BF16)

    def proj(i):
        return jnp.dot(xb, win_ref[:, i * d:(i + 1) * d], preferred_element_type=F32)

    u = proj(2) * proj(0)
    cw = cw_ref[...]
    row = lax.broadcasted_iota(jnp.int32, (ts, d), 0)
    convs = []
    for b in range(bb):
        ub = u[b * ts:(b + 1) * ts]
        hist = carry_ref[b]
        h1 = hist[HIST - 1:HIST]
        h2 = hist[HIST - 2:HIST - 1]
        um1 = jnp.where(row == 0, h1, pltpu.roll(ub, 1, 0))
        um2 = jnp.where(row == 0, h2, jnp.where(row == 1, h1, pltpu.roll(ub, 2, 0)))
        convs.append(cw[0:1] * um2 + cw[1:2] * um1 + cw[2:3] * ub)
        carry_ref[b] = ub[ts - HIST:ts]
        nc_ref[b] = ub[ts - HIST:ts]
    conv = convs[0] if bb == 1 else jnp.concatenate(convs, axis=0)
    yc = (proj(1) * conv).astype(BF16)
    mc = jax.nn.sigmoid(proj(6)) * jnp.dot(yc, wb0_ref[...], preferred_element_type=F32)
    mc_ref[...] = mc.astype(BF16).reshape(bb, ts, d)
    sga_ref[...] = jax.nn.sigmoid(proj(7)).astype(BF16).reshape(bb, ts, d)
    k = proj(4)
    k32_ref[...] = k.reshape(bb, ts, d)
    kb_ref[...] = k.astype(BF16).reshape(bb, ts, d)
    v = proj(5)
    v32_ref[...] = v.reshape(bb, ts, d)
    if transposed:
        nt = (((1,), (1,)), ((), ()))
        qt = lax.dot_general(wqvt_ref[0:d, :], xb, nt, preferred_element_type=F32)
        qt = (qt * (q_scale * LOG2_E)).astype(BF16)
        vt = lax.dot_general(wqvt_ref[d:2 * d, :], xb, nt, preferred_element_type=F32).astype(BF16)
        ta = q_ref.shape[3]
        for j in range(ts // ta):
            q_ref[0, j] = qt[:, j * ta:(j + 1) * ta]
            vb_ref[0, j] = vt[:, j * ta:(j + 1) * ta]
    else:
        q_ref[...] = (proj(3) * q_scale).astype(BF16).reshape(bb, ts, d)
        vb_ref[...] = v.astype(BF16).reshape(bb, ts, d)


def _proj_conv(x, c0, w_in, conv_w, wb0, wqvt, *, bb, ts, q_scale, transposed, ta=None):
    nb, length, d = x.shape
    assert nb % bb == 0 and length % ts == 0 and ts % HIST == 0
    assert w_in.shape == (d, 8 * d), "all eight projection sections must be d_model wide"
    assert not transposed or (bb == 1 and ts % ta == 0)
    blk = pl.BlockSpec((bb, ts, d), lambda b, s: (b, s, 0))
    hist_blk = pl.BlockSpec((bb, HIST, d), lambda b, s: (b, 0, 0))
    f32_out = jax.ShapeDtypeStruct((nb, length, d), F32)
    bf_out = jax.ShapeDtypeStruct((nb, length, d), BF16)
    if transposed:
        qv_blk = pl.BlockSpec((1, ts // ta, d, ta), lambda b, s: (b, s, 0, 0))
        qv_out = jax.ShapeDtypeStruct((nb, length // ta, d, ta), BF16)
    else:
        qv_blk, qv_out = blk, bf_out
    return pl.pallas_call(
        functools.partial(_proj_conv_kernel, bb=bb, ts=ts, d=d, q_scale=q_scale, transposed=transposed),
        grid=(nb // bb, length // ts),
        in_specs=[blk, hist_blk, _const_spec((d, 8 * d)), _const_spec((3, d)), _const_spec((d, d)),
                  _const_spec((2 * d, d))],
        out_specs=[blk, blk, qv_blk, blk, qv_blk, blk, blk, hist_blk],
        out_shape=[f32_out, f32_out, qv_out, bf_out, qv_out, bf_out, bf_out,
                   jax.ShapeDtypeStruct((nb, HIST, d), F32)],
        scratch_shapes=[pltpu.VMEM((bb, HIST, d), F32)],
        compiler_params=pltpu.CompilerParams(
            dimension_semantics=("parallel", "arbitrary"), vmem_limit_bytes=VMEM_LIMIT),
        name="proj_conv",
    )(x, c0, w_in, conv_w, wb0, wqvt)


def _pad_keys(x):
    pad = -x.shape[0] % LANES
    return x if pad == 0 else jnp.concatenate([x, jnp.zeros((pad, x.shape[1]), x.dtype)], axis=0)


def _attn_kernel(lqk_ref, g_ref, q_ref, ck_ref, cv_ref, kf_ref, vf_ref, o_ref,
                 *, tq, frame0, lam_init, dqk):
    qi = pl.program_id(2)
    lqk = lqk_ref[...]
    lam = (jnp.exp(jnp.sum(lqk[0:1] * lqk[1:2], axis=-1, keepdims=True))
           - jnp.exp(jnp.sum(lqk[2:3] * lqk[3:4], axis=-1, keepdims=True)) + lam_init)

    q = q_ref[0]
    lane = lax.broadcasted_iota(jnp.int32, q.shape, 1)
    zero = jnp.zeros_like(q)
    qs = jnp.concatenate([jnp.where(lane < dqk, q, zero), jnp.where(lane >= dqk, q, zero)], axis=0)

    def scores(k):
        return lax.dot_general(qs, k, (((1,), (1,)), ((), ())), preferred_element_type=F32)

    n_ctx = ck_ref.shape[1]
    s = scores(_pad_keys(ck_ref[0].astype(BF16)))
    if n_ctx % LANES:
        s = jnp.where(lax.broadcasted_iota(jnp.int32, s.shape, 1) < n_ctx, s, -jnp.inf)
    m = jnp.max(s, axis=-1, keepdims=True)
    p = jnp.exp(s - m)
    l = jnp.sum(p, axis=-1, keepdims=True)
    acc = jnp.dot(p.astype(BF16), _pad_keys(cv_ref[0].astype(BF16)), preferred_element_type=F32)

    def step(j, carry, masked):
        m, l, acc = carry
        start = pl.multiple_of(j * tq, tq)
        s = scores(_pad_keys(kf_ref[0, pl.ds(start, tq), :]))
        if masked:
            r = lax.broadcasted_iota(jnp.int32, s.shape, 0)
            c = lax.broadcasted_iota(jnp.int32, s.shape, 1)
            q_chunk = lax.shift_right_logical(frame0 + j * tq + jnp.where(r >= tq, r - tq, r), CHUNK_SHIFT)
            k_chunk = lax.shift_right_logical(frame0 + j * tq + c, CHUNK_SHIFT)
            s = jnp.where((k_chunk <= q_chunk) & (c < tq), s, -jnp.inf)
        m_new = jnp.maximum(m, jnp.max(s, axis=-1, keepdims=True))
        alpha = jnp.exp(m - m_new)
        p = jnp.exp(s - m_new)
        l = alpha * l + jnp.sum(p, axis=-1, keepdims=True)
        acc = alpha * acc + jnp.dot(p.astype(BF16), _pad_keys(vf_ref[0, pl.ds(start, tq), :]),
                                    preferred_element_type=F32)
        return m_new, l, acc

    carry = lax.fori_loop(0, qi, functools.partial(step, masked=False), (m, l, acc))
    m, l, acc = step(qi, carry, masked=True)

    o = acc[:tq] / l[:tq] - lam * (acc[tq:] / l[tq:])
    o = o * lax.rsqrt(jnp.mean(o * o, axis=-1, keepdims=True) + RMS_EPS) * g_ref[...] * (1.0 - lam_init)
    o_ref[0] = o.astype(o_ref.dtype)


def _attention(lqk, subln_g, q, ctx_k, ctx_v, kf, vf, *, n_heads, tq, frame0, lam_init):
    nb, length, d = q.shape
    dv = d // n_heads
    dqk = lqk.shape[1]
    assert dv == 2 * dqk and length % tq == 0
    assert tq % LANES == 0 or length == tq, "only a single (diagonal) frame tile may be lane-padded"
    n_ctx = ctx_k.shape[1]
    ctx_map = (lambda b, h, i: (b, 0, h)) if ctx_k.shape[0] == nb else (lambda b, h, i: (0, 0, h))
    q_blk = pl.BlockSpec((1, tq, dv), lambda b, h, i: (b, i, h))
    ctx_blk = pl.BlockSpec((1, n_ctx, dv), ctx_map)
    kv_blk = pl.BlockSpec((1, length, dv), lambda b, h, i: (b, 0, h))
    return pl.pallas_call(
        functools.partial(_attn_kernel, tq=tq, frame0=frame0, lam_init=lam_init, dqk=dqk),
        grid=(nb, n_heads, length // tq),
        in_specs=[pl.BlockSpec((4, dqk), lambda b, h, i: (0, 0)),
                  pl.BlockSpec((1, dv), lambda b, h, i: (0, 0)),
                  q_blk, ctx_blk, ctx_blk, kv_blk, kv_blk],
        out_specs=q_blk,
        out_shape=jax.ShapeDtypeStruct((nb, length, d), BF16),
        compiler_params=pltpu.CompilerParams(
            dimension_semantics=("parallel", "parallel", "arbitrary"), vmem_limit_bytes=VMEM_LIMIT),
        name="diff_attention",
    )(lqk, subln_g, q, ctx_k, ctx_v, kf, vf)


def _attn_t_kernel(lqk_ref, g_ref, bias_ref, qt_ref, mk_ref, mvt_ref, kf_ref, vt_ref, o_ref,
                   *, tq, n_tiles, lam_init, dqk):
    lqk = lqk_ref[...]
    lam = (jnp.exp(jnp.sum(lqk[0:1] * lqk[1:2], axis=-1, keepdims=True))
           - jnp.exp(jnp.sum(lqk[2:3] * lqk[3:4], axis=-1, keepdims=True)) + lam_init)

    dv = qt_ref.shape[2]
    row = lax.broadcasted_iota(jnp.int32, (dv, tq), 0)
    zero = jnp.zeros((dv, tq), BF16)
    ones = jnp.ones((BF16_SUBLANES, tq), BF16)

    def query_tile(q):
        n = q + 2
        qt = qt_ref[0, q]
        w = jnp.concatenate([jnp.where(row < dqk, qt, zero), jnp.where(row >= dqk, qt, zero)], axis=1)

        def scores(t):
            if t == 0:
                return jnp.dot(mk_ref[...], w, preferred_element_type=F32) + bias_ref[BIAS_META]
            s = jnp.dot(kf_ref[0, (t - 1) * tq:t * tq, :], w, preferred_element_type=F32)
            return s + bias_ref[BIAS_DIAG] if t == n - 1 else s

        s = scores(0)
        for t in range(n):
            s_next = scores(t + 1) if t + 1 < n else None
            vt = jnp.concatenate([mvt_ref[...] if t == 0 else vt_ref[0, t - 1], ones], axis=0)
            s_max = jnp.max(s, axis=0, keepdims=True)
            if t == 0:
                m = s_max
                acc = jnp.dot(vt, jnp.exp2(s - m).astype(BF16), preferred_element_type=F32)
            else:
                m_new = jnp.maximum(m, s_max)
                acc = jnp.exp2(m - m_new) * acc + jnp.dot(vt, jnp.exp2(s - m_new).astype(BF16),
                                                          preferred_element_type=F32)
                m = m_new
            s = s_next

        l = acc[dv:dv + 1]
        ot = acc[:dv, :tq] / l[:, :tq] - lam * (acc[:dv, tq:] / l[:, tq:])
        ot = ot * lax.rsqrt(jnp.mean(ot * ot, axis=0, keepdims=True) + RMS_EPS)
        o_ref[0, q * tq:(q + 1) * tq, :] = (ot.T * g_ref[...] * (1.0 - lam_init)).astype(o_ref.dtype)

    for q in range(n_tiles):
        query_tile(q)


def _attention_bias(tq, n_meta):
    r = jnp.arange(tq, dtype=jnp.int32)[:, None]
    c = jnp.arange(2 * tq, dtype=jnp.int32)[None, :] % tq
    meta = jnp.broadcast_to(r < n_meta, (tq, 2 * tq))
    diag = (r // CHUNK) <= (c // CHUNK)
    neg = jnp.float32(-jnp.inf)
    return jnp.stack([jnp.where(meta, 0.0, neg), jnp.where(diag, 0.0, neg)])


def _attention_t(lqk, subln_g, qt, mk, mvt, kf, vt, *, n_heads, n_meta, lam_init):
    nb, n_tiles, d, tq = qt.shape
    length = n_tiles * tq
    dv = d // n_heads
    dqk = lqk.shape[1]
    assert dv == 2 * dqk and kf.shape == (nb, length, d) and vt.shape == qt.shape
    assert mk.shape == (tq, d) and mvt.shape == (d, tq) and n_meta <= tq and tq % CHUNK == 0
    return pl.pallas_call(
        functools.partial(_attn_t_kernel, tq=tq, n_tiles=n_tiles, lam_init=lam_init, dqk=dqk),
        grid=(nb, n_heads),
        in_specs=[pl.BlockSpec((4, dqk), lambda b, h: (0, 0)),
                  pl.BlockSpec((1, dv), lambda b, h: (0, 0)),
                  _const_spec((2, tq, 2 * tq)),
                  pl.BlockSpec((1, n_tiles, dv, tq), lambda b, h: (b, 0, h, 0)),
                  pl.BlockSpec((tq, dv), lambda b, h: (0, h)),
                  pl.BlockSpec((dv, tq), lambda b, h: (h, 0)),
                  pl.BlockSpec((1, length, dv), lambda b, h: (b, 0, h)),
                  pl.BlockSpec((1, n_tiles, dv, tq), lambda b, h: (b, 0, h, 0))],
        out_specs=pl.BlockSpec((1, length, dv), lambda b, h: (b, 0, h)),
        out_shape=jax.ShapeDtypeStruct((nb, length, d), BF16),
        compiler_params=pltpu.CompilerParams(
            dimension_semantics=("parallel", "parallel"), vmem_limit_bytes=VMEM_LIMIT),
        name="diff_attention_t",
    )(lqk, subln_g, _attention_bias(tq, n_meta), qt, mk, mvt, kf, vt)


def _tail_kernel(x_ref, mc_ref, sga_ref, ya_ref, wb1_ref, wout_ref, g_ref, b_ref, wrt_ref, brt_ref,
                 h32_ref, route_ref, counts_ref, *, alpha, n_experts, n_groups, n_sub):
    @pl.when(pl.program_id(0) == 0)
    def _():
        counts_ref[...] = jnp.zeros_like(counts_ref)

    tm = x_ref.shape[0]
    sub = tm // n_sub
    lane = lax.broadcasted_iota(jnp.int32, (sub, ROUTE_LANES), 1)
    big = jnp.int32(ROUTE_LANES)
    neg = -jnp.inf
    per_group = n_experts // n_groups
    r = lax.broadcasted_iota(jnp.int32, (sub, sub), 0)
    c = lax.broadcasted_iota(jnp.int32, (sub, sub), 1)
    earlier = jnp.where(c < r, 1.0, 0.0).astype(BF16)

    def first_max(v):
        vmax = jnp.max(v, axis=-1, keepdims=True)
        return vmax, jnp.min(jnp.where(v == vmax, lane, big), axis=-1, keepdims=True)

    counts = counts_ref[...]
    for g in range(n_sub):
        rows = slice(g * sub, (g + 1) * sub)
        merged = mc_ref[rows].astype(F32) + sga_ref[rows].astype(F32) * jnp.dot(
            ya_ref[rows], wb1_ref[...], preferred_element_type=F32)
        y = alpha * x_ref[rows] + jnp.dot(merged.astype(BF16), wout_ref[...], preferred_element_type=F32)
        h = _layer_norm(y, g_ref[...], b_ref[...])
        h32_ref[rows] = h

        h_hi = h.astype(BF16)
        h_lo = (h - h_hi.astype(F32)).astype(BF16)
        hw = jnp.dot(h_hi, wrt_ref[...], preferred_element_type=F32)
        logits = (hw[:, :ROUTE_LANES] + hw[:, ROUTE_LANES:]
                  + jnp.dot(h_lo, wrt_ref[:, :ROUTE_LANES], preferred_element_type=F32)) + brt_ref[...]

        is_group = (lane >= n_experts) & (lane < n_experts + n_groups)
        lg = jnp.where(is_group, logits, neg)
        eg = jnp.exp(lg - jnp.max(lg, axis=-1, keepdims=True))
        p_group = jnp.where(is_group, eg / jnp.sum(eg, axis=-1, keepdims=True), neg)
        p_g, g_lane = first_max(p_group)
        g_idx = g_lane - n_experts
        lo = g_idx * per_group
        le = jnp.where((lane >= lo) & (lane < lo + per_group), logits, neg)
        v1, i1 = first_max(le)
        v2, i2 = first_max(jnp.where(lane == i1, neg, le))
        e2 = jnp.exp(v2 - v1)
        w1 = p_g / (1.0 + e2)
        w2 = p_g * e2 / (1.0 + e2)

        onehot = jnp.where((lane == i1) | (lane == i2), 1.0, 0.0).astype(BF16)
        before = jnp.dot(earlier, onehot, preferred_element_type=F32) + counts
        r1 = jnp.sum(jnp.where(lane == i1, before, 0.0), axis=-1, keepdims=True)
        r2 = jnp.sum(jnp.where(lane == i2, before, 0.0), axis=-1, keepdims=True)
        counts = counts + jnp.sum(onehot.astype(F32), axis=0, keepdims=True)

        route = jnp.zeros_like(logits)
        for k, col in ((ROUTE_E0, i1.astype(F32)), (ROUTE_E1, i2.astype(F32)), (ROUTE_W0, w1), (ROUTE_W1, w2),
                       (ROUTE_R0, r1), (ROUTE_R1, r2)):
            route = jnp.where(lane == k, col, route)
        route_ref[rows] = route
    counts_ref[...] = counts


def _tail(x, mc, sga, ya, wb1, w_out, ln_g, ln_b, w_rt, b_rt, *, tm, alpha, n_experts, n_groups):
    t, d = x.shape
    assert t % tm == 0
    n_sub = 2 if tm % (2 * LANES) == 0 else 1
    row = lambda w: pl.BlockSpec((tm, w), lambda i: (i, 0))
    return pl.pallas_call(
        functools.partial(_tail_kernel, alpha=alpha, n_experts=n_experts, n_groups=n_groups, n_sub=n_sub),
        grid=(t // tm,),
        in_specs=[row(d), row(d), row(d), row(d), _const_spec((d, d)), _const_spec((d, d)),
                  _const_spec((1, d)), _const_spec((1, d)),
                  _const_spec((d, 2 * ROUTE_LANES)), _const_spec((1, ROUTE_LANES))],
        out_specs=[row(d), row(ROUTE_LANES), pl.BlockSpec((1, ROUTE_LANES), lambda i: (0, 0))],
        out_shape=[jax.ShapeDtypeStruct((t, d), F32), jax.ShapeDtypeStruct((t, ROUTE_LANES), F32),
                   jax.ShapeDtypeStruct((1, ROUTE_LANES), F32)],
        compiler_params=pltpu.CompilerParams(
            dimension_semantics=("arbitrary",), vmem_limit_bytes=VMEM_LIMIT),
        name="merge_ln_router",
    )(x, mc, sga, ya, wb1, w_out, ln_g, ln_b, w_rt, b_rt)


def _row_copy(src_ref, src_row, dst_ref, dst_row, sem):
    return pltpu.make_async_copy(src_ref.at[pl.ds(src_row, 1)], dst_ref.at[pl.ds(dst_row, 1)], sem)


def _moe_scatter_kernel(pos_ref, pad_rows_ref, n_pad_ref, n_used_ref, h_ref, xs_ref, zero_ref, sem, pad_sem,
                        *, tm):
    @pl.when(pl.program_id(0) == 0)
    def _():
        zero_ref[...] = jnp.zeros_like(zero_ref)
        tr = zero_ref.shape[0]
        n_tiles = xs_ref.shape[0] // tr

        def tile_copy(r):
            return pltpu.make_async_copy(zero_ref, xs_ref.at[pl.ds(pl.multiple_of(r * tr, tr), tr)], pad_sem)

        def issue_pad(j, _):
            _row_copy(zero_ref, 0, xs_ref, pad_rows_ref[j], pad_sem).start()
            return 0

        def wait_pad(j, _):
            _row_copy(zero_ref, 0, xs_ref, 0, pad_sem).wait()
            return 0

        lax.fori_loop(0, n_pad_ref[0], issue_pad, 0)
        lax.fori_loop(n_used_ref[0], n_tiles, lambda r, _: (tile_copy(r).start(), 0)[1], 0)
        lax.fori_loop(0, n_pad_ref[0], wait_pad, 0)
        lax.fori_loop(n_used_ref[0], n_tiles, lambda r, _: (tile_copy(r).wait(), 0)[1], 0)

    def issue(t, _):
        for k in range(TOP_K):
            _row_copy(h_ref, t, xs_ref, pos_ref[0, 0, k * tm + t], sem).start()
        return 0

    lax.fori_loop(0, tm, issue, 0)
    for _ in range(TOP_K):
        pltpu.make_async_copy(h_ref, xs_ref.at[pl.ds(0, tm)], sem).wait()


def _moe_scatter(pos, pad_rows, n_pad, n_used, h32, n_rows, *, tm, tr):
    t, w = h32.shape
    smem = pl.BlockSpec(memory_space=pltpu.SMEM)
    return pl.pallas_call(
        functools.partial(_moe_scatter_kernel, tm=tm),
        grid=(t // tm,),
        in_specs=[pl.BlockSpec((1, 1, TOP_K * tm), lambda i: (i, 0, 0), memory_space=pltpu.SMEM),
                  smem, smem, smem, pl.BlockSpec((tm, w), lambda i: (i, 0))],
        out_specs=pl.BlockSpec(memory_space=pl.ANY),
        out_shape=jax.ShapeDtypeStruct((n_rows, w), h32.dtype),
        scratch_shapes=[pltpu.VMEM((tr, w), h32.dtype), pltpu.SemaphoreType.DMA(()), pltpu.SemaphoreType.DMA(())],
        compiler_params=pltpu.CompilerParams(
            dimension_semantics=("arbitrary",), vmem_limit_bytes=VMEM_LIMIT, has_side_effects=True),
        name="moe_scatter",
    )(pos, pad_rows, n_pad, n_used, h32)


def _moe_experts_kernel(tile_expert_ref, n_used_ref, xs_ref, wgu_ref, wd_ref, ys_ref, *, d_expert):
    del tile_expert_ref
    used = pl.program_id(0) < n_used_ref[0]

    @pl.when(used)
    def _():
        ab = jnp.dot(xs_ref[...].astype(BF16), wgu_ref[0], preferred_element_type=F32)
        a = ab[:, :d_expert]
        act = (a * jax.nn.sigmoid(a) * ab[:, d_expert:]).astype(BF16)
        ys_ref[...] = jnp.dot(act, wd_ref[0], preferred_element_type=F32)

    @pl.when(jnp.logical_not(used))
    def _():
        ys_ref[...] = jnp.zeros_like(ys_ref)


def _moe_experts(tile_expert, n_used, xs, wgu, wd, *, tr):
    n_rows, w = xs.shape
    _, d, two_de = wgu.shape
    assert d == w
    last_used = lambda r, nu: jnp.minimum(r, nu[0] - 1)
    return pl.pallas_call(
        functools.partial(_moe_experts_kernel, d_expert=two_de // 2),
        grid_spec=pltpu.PrefetchScalarGridSpec(
            num_scalar_prefetch=2,
            grid=(n_rows // tr,),
            in_specs=[pl.BlockSpec((tr, w), lambda r, te, nu: (last_used(r, nu), 0)),
                      pl.BlockSpec((1, d, two_de), lambda r, te, nu: (te[r], 0, 0)),
                      pl.BlockSpec((1, two_de // 2, d), lambda r, te, nu: (te[r], 0, 0))],
            out_specs=pl.BlockSpec((tr, w), lambda r, te, nu: (r, 0))),
        out_shape=jax.ShapeDtypeStruct((n_rows, w), xs.dtype),
        compiler_params=pltpu.CompilerParams(
            dimension_semantics=("arbitrary",), vmem_limit_bytes=VMEM_LIMIT),
        name="moe_experts",
    )(tile_expert, n_used, xs, wgu, wd)


def _moe_combine_kernel(pos_ref, route_ref, h_ref, g_ref, b_ref, ys_ref, o_ref, y_buf, sem, *, tm, alpha):
    def issue(t, _):
        for k in range(TOP_K):
            _row_copy(ys_ref, pos_ref[0, 0, k * tm + t], y_buf.at[k], t, sem).start()
        return 0

    lax.fori_loop(0, tm, issue, 0)
    for k in range(TOP_K):
        pltpu.make_async_copy(ys_ref.at[pl.ds(0, tm)], y_buf.at[k], sem).wait()

    route = route_ref[...]
    lane = lax.broadcasted_iota(jnp.int32, route.shape, 1)
    f = jnp.zeros_like(h_ref)
    for k, w_lane in enumerate((ROUTE_W0, ROUTE_W1)):
        gate = jnp.sum(jnp.where(lane == w_lane, route, 0.0), axis=-1, keepdims=True)
        f = f + gate * y_buf[k]
    o_ref[...] = _layer_norm(alpha * h_ref[...] + f, g_ref[...], b_ref[...])


def _moe_combine(pos, route, h32, ys, ln_g, ln_b, *, tm, alpha):
    t, d = h32.shape
    return pl.pallas_call(
        functools.partial(_moe_combine_kernel, tm=tm, alpha=alpha),
        grid=(t // tm,),
        in_specs=[pl.BlockSpec((1, 1, TOP_K * tm), lambda i: (i, 0, 0), memory_space=pltpu.SMEM),
                  pl.BlockSpec((tm, ROUTE_LANES), lambda i: (i, 0)),
                  pl.BlockSpec((tm, d), lambda i: (i, 0)),
                  _const_spec((1, d)), _const_spec((1, d)),
                  pl.BlockSpec(memory_space=pl.ANY)],
        out_specs=pl.BlockSpec((tm, d), lambda i: (i, 0)),
        out_shape=jax.ShapeDtypeStruct((t, d), F32),
        scratch_shapes=[pltpu.VMEM((TOP_K, tm) + ys.shape[1:], ys.dtype), pltpu.SemaphoreType.DMA(())],
        compiler_params=pltpu.CompilerParams(
            dimension_semantics=("arbitrary",), vmem_limit_bytes=VMEM_LIMIT),
        name="moe_combine",
    )(pos, route, h32, ln_g, ln_b, ys)


def _moe(h32, route, counts, wgu, wd, ln_g, ln_b, *, tm, alpha):
    t, d = h32.shape
    n_experts = wgu.shape[0]
    tr = min(MOE_ROW_TILE, max(LANES, TOP_K * t // n_experts))
    assert t % tm == 0
    n_tiles = (TOP_K * t + n_experts * (tr - 1)) // tr

    def segment_of(i, seg_ends):
        return jnp.minimum(jnp.sum(i[:, None] >= seg_ends[None, :], axis=1), n_experts - 1).astype(jnp.int32)

    experts = jnp.arange(n_experts, dtype=jnp.int32)
    cnt = counts[0, :n_experts].astype(jnp.int32)
    padded = (cnt + tr - 1) // tr * tr
    ends = jnp.cumsum(padded)
    starts = ends - padded
    expert = route[:, ROUTE_E0:ROUTE_E1 + 1].astype(jnp.int32)
    rank = route[:, ROUTE_R0:ROUTE_R1 + 1].astype(jnp.int32)
    pos = jnp.sum(jnp.where(expert[..., None] == experts, starts, 0), axis=-1) + rank
    pos = pos.reshape(t // tm, tm, TOP_K).transpose(0, 2, 1).reshape(t // tm, 1, TOP_K * tm)
    tile_expert = segment_of(jnp.arange(n_tiles, dtype=jnp.int32) * tr, ends)
    n_used = ends[-1:] // tr
    pad_ends = jnp.cumsum(padded - cnt)
    i = jnp.arange(n_experts * (tr - 1), dtype=jnp.int32)
    seg = segment_of(i, pad_ends)
    pad_rows = jnp.minimum((starts + cnt)[seg] + i - (pad_ends - (padded - cnt))[seg], n_tiles * tr - 1)

    xs = _moe_scatter(pos, pad_rows, pad_ends[-1:], n_used, h32, n_tiles * tr, tm=tm, tr=tr)
    ys = _moe_experts(tile_expert, n_used, xs, wgu, wd, tr=tr)
    return _moe_combine(pos, route, h32, ys, ln_g, ln_b, tm=tm, alpha=alpha)


def _pick_tile(n, target):
    t = min(n, target)
    while n % t:
        t //= 2
    return t


def kernel(x_prompt, x_sample, cache_k, cache_v, state_conv, meta_tokens, w_in, conv_w, lambda_qk, subln_g, w_branch, w_out, ln1_g, ln1_b, w_group, b_group, w_router, b_router, w_gate_up, w_down, ln2_g, ln2_b):
    depth = w_in.shape[0]
    assert depth == 1, "single-layer step only"
    bp, seq, d = x_prompt.shape
    bs, s_len, _ = x_sample.shape
    n_meta = meta_tokens.shape[0]
    n_heads = cache_k.shape[3]
    dqk = cache_k.shape[4] // 2
    past = cache_k.shape[2] - n_meta
    n_groups = w_group.shape[-1]
    n_experts = w_router.shape[-1]
    assert n_experts + n_groups <= ROUTE_LANES
    q_scale = dqk ** -0.5
    assert math.frexp(q_scale)[0] == 0.5, "the score scale is folded into q; exact only for powers of two"
    alpha = (2.0 * depth) ** 0.25
    lam_init = 0.8 - 0.6 * math.exp(-0.3 * 0)

    w_in_b = w_in[0].astype(BF16)
    wb = w_branch[0].astype(BF16)
    w_out_b = w_out[0].astype(BF16)
    wgu_b = w_gate_up[0].astype(BF16)
    wd_b = w_down[0].astype(BF16)
    w_rt = jnp.zeros((d, ROUTE_LANES), F32).at[:, :n_experts].set(w_router[0]).at[
        :, n_experts:n_experts + n_groups].set(w_group[0])
    w_rt_hi = w_rt.astype(BF16)
    w_rt = jnp.concatenate([w_rt_hi, (w_rt - w_rt_hi.astype(F32)).astype(BF16)], axis=1)
    b_rt = jnp.zeros((1, ROUTE_LANES), F32).at[0, :n_experts].set(b_router[0]).at[
        0, n_experts:n_experts + n_groups].set(b_group[0])

    def hist_rows(rows):
        return jnp.pad(rows, ((0, 0), (HIST - rows.shape[1], 0), (0, 0)))

    wqvt = jnp.concatenate([w_in_b[:, 3 * d:4 * d].T, w_in_b[:, 5 * d:6 * d].T], axis=0)
    proj = functools.partial(_proj_conv, w_in=w_in_b, conv_w=conv_w[0], wb0=wb[0], wqvt=wqvt, q_scale=q_scale)
    attn = functools.partial(_attention, lambda_qk[0], subln_g, n_heads=n_heads, lam_init=lam_init)
    tail = functools.partial(_tail, wb1=wb[1], w_out=w_out_b, ln_g=ln1_g, ln_b=ln1_b, w_rt=w_rt, b_rt=b_rt,
                             alpha=alpha, n_experts=n_experts, n_groups=n_groups)
    moe = functools.partial(_moe, wgu=wgu_b, wd=wd_b, ln_g=ln2_g, ln_b=ln2_b, alpha=alpha)

    mk32, mv32, _, mkb, mvb, _, _, mnc = proj(
        meta_tokens[None], jnp.zeros((1, HIST, d), F32), bb=1, ts=n_meta, transposed=False)
    ta = _pick_tile(seq, ATTN_TILE)
    mk_pad = jnp.pad(mkb[0], ((0, ta - n_meta), (0, 0)))
    mvt_pad = jnp.pad(mvb[0].T, ((0, 0), (0, ta - n_meta)))
    k32, v32, qt, kb, vt, mc, sga, nc = proj(
        x_prompt, jnp.broadcast_to(mnc, (bp, HIST, d)), bb=1, ts=_pick_tile(seq, PROJ_TILE), transposed=True, ta=ta)
    ya = _attention_t(lambda_qk[0], subln_g, qt, mk_pad, mvt_pad, kb, vt,
                      n_heads=n_heads, n_meta=n_meta, lam_init=lam_init)
    t_p = bp * seq
    tm = _pick_tile(t_p, 512)
    routed = tail(x_prompt.reshape(t_p, d), mc.reshape(t_p, d), sga.reshape(t_p, d), ya.reshape(t_p, d), tm=tm)
    y_prompt = moe(*routed, tm=tm).reshape(bp, seq, d)

    sk32, sv32, sqb, skb, svb, smc, ssga, snc = proj(
        x_sample, hist_rows(state_conv[0]), bb=bs, ts=s_len, transposed=False)
    sya = attn(sqb, cache_k[0].reshape(bs, n_meta + past, d), cache_v[0].reshape(bs, n_meta + past, d),
               skb, svb, tq=s_len, frame0=past)
    t_s = bs * s_len
    srouted = tail(x_sample.reshape(t_s, d), smc.reshape(t_s, d), ssga.reshape(t_s, d), sya.reshape(t_s, d),
                   tm=t_s)
    y_sample = moe(*srouted, tm=t_s).reshape(bs, s_len, d)

    def with_meta(m, f):
        full = jnp.concatenate([jnp.broadcast_to(m, (bp, n_meta, d)), f], axis=1)
        return full.reshape(1, bp, n_meta + seq, n_heads, d // n_heads)

    return (y_prompt, y_sample,
            with_meta(mk32, k32), with_meta(mv32, v32), nc[None, :, HIST - 2:],
            sk32.reshape(1, bs, s_len, n_heads, d // n_heads),
            sv32.reshape(1, bs, s_len, n_heads, d // n_heads), snc[None, :, HIST - 2:])
```

```python
import functools
import math

import jax
import jax.numpy as jnp
from jax import lax
from jax.experimental import pallas as pl
from jax.experimental.pallas import tpu as pltpu

CHUNK = 64
CHUNK_SHIFT = CHUNK.bit_length() - 1
assert 1 << CHUNK_SHIFT == CHUNK
LANES = 128
BF16_SUBLANES = 16
LOG2_E = math.log2(math.e)
LN_EPS = 1e-5
RMS_EPS = 1e-5
HIST = 8
ROUTE_LANES = 128
ROUTE_E0, ROUTE_E1, ROUTE_W0, ROUTE_W1, ROUTE_R0, ROUTE_R1 = range(6)
TOP_K = 2
MOE_ROW_TILE = 512
PROJ_TILE = 512
ATTN_TILE = 256
VMEM_LIMIT = 52 * 1024 * 1024

F32 = jnp.float32
BF16 = jnp.bfloat16


def _const_spec(shape):
    return pl.BlockSpec(shape, lambda *_: (0,) * len(shape), pipeline_mode=pl.Buffered(1))


def _layer_norm(x, g, b):
    mu = jnp.mean(x, axis=-1, keepdims=True)
    xc = x - mu
    var = jnp.mean(xc * xc, axis=-1, keepdims=True)
    return xc * lax.rsqrt(var + LN_EPS) * g + b


def _proj_conv_kernel(x_ref, c0_ref, win_ref, cw_ref, wb0_ref, wqt_ref,
                      k32_ref, v32_ref, q_ref, kb_ref, vb_ref, mc_ref, sga_ref, nc_ref,
                      carry_ref, *, bb, ts, d, q_scale, transposed):
    @pl.when(pl.program_id(1) == 0)
    def _():
        carry_ref[...] = c0_ref[...]

    xb = x_ref[...].reshape(bb * ts, d).astype(BF16)

    def proj(i):
        return jnp.dot(xb, win_ref[:, i * d:(i + 1) * d], preferred_element_type=F32)

    u = proj(2) * proj(0)
    cw = cw_ref[...]
    row = lax.broadcasted_iota(jnp.int32, (ts, d), 0)
    convs = []
    for b in range(bb):
        ub = u[b * ts:(b + 1) * ts]
        hist = carry_ref[b]
        h1 = hist[HIST - 1:HIST]
        h2 = hist[HIST - 2:HIST - 1]
        um1 = jnp.where(row == 0, h1, pltpu.roll(ub, 1, 0))
        um2 = jnp.where(row == 0, h2, jnp.where(row == 1, h1, pltpu.roll(ub, 2, 0)))
        convs.append(cw[0:1] * um2 + cw[1:2] * um1 + cw[2:3] * ub)
        carry_ref[b] = ub[ts - HIST:ts]
        nc_ref[b] = ub[ts - HIST:ts]
    conv = convs[0] if bb == 1 else jnp.concatenate(convs, axis=0)
    yc = (proj(1) * conv).astype(BF16)
    mc = jax.nn.sigmoid(proj(6)) * jnp.dot(yc, wb0_ref[...], preferred_element_type=F32)
    mc_ref[...] = mc.astype(BF16).reshape(bb, ts, d)
    sga_ref[...] = jax.nn.sigmoid(proj(7)).astype(BF16).reshape(bb, ts, d)
    k = proj(4)
    k32_ref[...] = k.reshape(bb, ts, d)
    kb_ref[...] = k.astype(BF16).reshape(bb, ts, d)
    v = proj(5)
    v32_ref[...] = v.reshape(bb, ts, d)
    if transposed:
        nt = (((1,), (1,)), ((), ()))
        qt = lax.dot_general(wqt_ref[...], xb, nt, preferred_element_type=F32)
        qt = (qt * (q_scale * LOG2_E)).astype(BF16)
        vt = v.T.astype(BF16)
        ta = q_ref.shape[3]
        for j in range(ts // ta):
            q_ref[0, j] = qt[:, j * ta:(j + 1) * ta]
            vb_ref[0, j] = vt[:, j * ta:(j + 1) * ta]
    else:
        q_ref[...] = (proj(3) * q_scale).astype(BF16).reshape(bb, ts, d)
        vb_ref[...] = v.astype(BF16).reshape(bb, ts, d)


def _proj_conv(x, c0, w_in, conv_w, wb0, wqt, *, bb, ts, q_scale, transposed, ta=None):
    nb, length, d = x.shape
    assert nb % bb == 0 and length % ts == 0 and ts % HIST == 0
    assert w_in.shape == (d, 8 * d), "all eight projection sections must be d_model wide"
    assert not transposed or (bb == 1 and ts % ta == 0)
    blk = pl.BlockSpec((bb, ts, d), lambda b, s: (b, s, 0))
    hist_blk = pl.BlockSpec((bb, HIST, d), lambda b, s: (b, 0, 0))
    f32_out = jax.ShapeDtypeStruct((nb, length, d), F32)
    bf_out = jax.ShapeDtypeStruct((nb, length, d), BF16)
    if transposed:
        qv_blk = pl.BlockSpec((1, ts // ta, d, ta), lambda b, s: (b, s, 0, 0))
        qv_out = jax.ShapeDtypeStruct((nb, length // ta, d, ta), BF16)
    else:
        qv_blk, qv_out = blk, bf_out
    return pl.pallas_call(
        functools.partial(_proj_conv_kernel, bb=bb, ts=ts, d=d, q_scale=q_scale, transposed=transposed),
        grid=(nb // bb, length // ts),
        in_specs=[blk, hist_blk, _const_spec((d, 8 * d)), _const_spec((3, d)), _const_spec((d, d)),
                  _const_spec((d, d))],
        out_specs=[blk, blk, qv_blk, blk, qv_blk, blk, blk, hist_blk],
        out_shape=[f32_out, f32_out, qv_out, bf_out, qv_out, bf_out, bf_out,
                   jax.ShapeDtypeStruct((nb, HIST, d), F32)],
        scratch_shapes=[pltpu.VMEM((bb, HIST, d), F32)],
        compiler_params=pltpu.CompilerParams(
            dimension_semantics=("parallel", "arbitrary"), vmem_limit_bytes=VMEM_LIMIT),
        name="proj_conv",
    )(x, c0, w_in, conv_w, wb0, wqt)


def _pad_keys(x):
    pad = -x.shape[0] % LANES
    return x if pad == 0 else jnp.concatenate([x, jnp.zeros((pad, x.shape[1]), x.dtype)], axis=0)


def _attn_kernel(lqk_ref, g_ref, q_ref, ck_ref, cv_ref, kf_ref, vf_ref, o_ref,
                 *, tq, frame0, lam_init, dqk):
    qi = pl.program_id(2)
    lqk = lqk_ref[...]
    lam = (jnp.exp(jnp.sum(lqk[0:1] * lqk[1:2], axis=-1, keepdims=True))
           - jnp.exp(jnp.sum(lqk[2:3] * lqk[3:4], axis=-1, keepdims=True)) + lam_init)

    q = q_ref[0]
    lane = lax.broadcasted_iota(jnp.int32, q.shape, 1)
    zero = jnp.zeros_like(q)
    qs = jnp.concatenate([jnp.where(lane < dqk, q, zero), jnp.where(lane >= dqk, q, zero)], axis=0)

    def scores(k):
        return lax.dot_general(qs, k, (((1,), (1,)), ((), ())), preferred_element_type=F32)

    n_ctx = ck_ref.shape[1]
    s = scores(_pad_keys(ck_ref[0].astype(BF16)))
    if n_ctx % LANES:
        s = jnp.where(lax.broadcasted_iota(jnp.int32, s.shape, 1) < n_ctx, s, -jnp.inf)
    m = jnp.max(s, axis=-1, keepdims=True)
    p = jnp.exp(s - m)
    l = jnp.sum(p, axis=-1, keepdims=True)
    acc = jnp.dot(p.astype(BF16), _pad_keys(cv_ref[0].astype(BF16)), preferred_element_type=F32)

    def step(j, carry, masked):
        m, l, acc = carry
        start = pl.multiple_of(j * tq, tq)
        s = scores(_pad_keys(kf_ref[0, pl.ds(start, tq), :]))
        if masked:
            r = lax.broadcasted_iota(jnp.int32, s.shape, 0)
            c = lax.broadcasted_iota(jnp.int32, s.shape, 1)
            q_chunk = lax.shift_right_logical(frame0 + j * tq + jnp.where(r >= tq, r - tq, r), CHUNK_SHIFT)
            k_chunk = lax.shift_right_logical(frame0 + j * tq + c, CHUNK_SHIFT)
            s = jnp.where((k_chunk <= q_chunk) & (c < tq), s, -jnp.inf)
        m_new = jnp.maximum(m, jnp.max(s, axis=-1, keepdims=True))
        alpha = jnp.exp(m - m_new)
        p = jnp.exp(s - m_new)
        l = alpha * l + jnp.sum(p, axis=-1, keepdims=True)
        acc = alpha * acc + jnp.dot(p.astype(BF16), _pad_keys(vf_ref[0, pl.ds(start, tq), :]),
                                    preferred_element_type=F32)
        return m_new, l, acc

    carry = lax.fori_loop(0, qi, functools.partial(step, masked=False), (m, l, acc))
    m, l, acc = step(qi, carry, masked=True)

    o = acc[:tq] / l[:tq] - lam * (acc[tq:] / l[tq:])
    o = o * lax.rsqrt(jnp.mean(o * o, axis=-1, keepdims=True) + RMS_EPS) * g_ref[...] * (1.0 - lam_init)
    o_ref[0] = o.astype(o_ref.dtype)


def _attention(lqk, subln_g, q, ctx_k, ctx_v, kf, vf, *, n_heads, tq, frame0, lam_init):
    nb, length, d = q.shape
    dv = d // n_heads
    dqk = lqk.shape[1]
    assert dv == 2 * dqk and length % tq == 0
    assert tq % LANES == 0 or length == tq, "only a single (diagonal) frame tile may be lane-padded"
    n_ctx = ctx_k.shape[1]
    assert ctx_k.shape == ctx_v.shape == (nb, n_ctx, d)
    q_blk = pl.BlockSpec((1, tq, dv), lambda b, h, i: (b, i, h))
    ctx_blk = pl.BlockSpec((1, n_ctx, dv), lambda b, h, i: (b, 0, h))
    kv_blk = pl.BlockSpec((1, length, dv), lambda b, h, i: (b, 0, h))
    return pl.pallas_call(
        functools.partial(_attn_kernel, tq=tq, frame0=frame0, lam_init=lam_init, dqk=dqk),
        grid=(nb, n_heads, length // tq),
        in_specs=[pl.BlockSpec((4, dqk), lambda b, h, i: (0, 0)),
                  pl.BlockSpec((1, dv), lambda b, h, i: (0, 0)),
                  q_blk, ctx_blk, ctx_blk, kv_blk, kv_blk],
        out_specs=q_blk,
        out_shape=jax.ShapeDtypeStruct((nb, length, d), BF16),
        compiler_params=pltpu.CompilerParams(
            dimension_semantics=("parallel", "parallel", "arbitrary"), vmem_limit_bytes=VMEM_LIMIT),
        name="diff_attention",
    )(lqk, subln_g, q, ctx_k, ctx_v, kf, vf)


def _attn_t_kernel(lqk_ref, g_ref, mbias_ref, dbias_ref, qt_ref, mk_ref, mvt_ref, kf_ref, vt_ref, o_ref,
                   *, tq, n_tiles, lam_init, dqk):
    lqk = lqk_ref[...]
    lam = (jnp.exp(jnp.sum(lqk[0:1] * lqk[1:2], axis=-1, keepdims=True))
           - jnp.exp(jnp.sum(lqk[2:3] * lqk[3:4], axis=-1, keepdims=True)) + lam_init)

    dv = qt_ref.shape[2]
    row = lax.broadcasted_iota(jnp.int32, (dv, tq), 0)
    zero = jnp.zeros((dv, tq), BF16)
    ones = jnp.ones((BF16_SUBLANES, tq), BF16)
    meta_fill = jnp.zeros((LANES - mk_ref.shape[0], 2 * tq), BF16)

    def query_tile(q):
        n = q + 2
        qt = qt_ref[0, q]
        w = jnp.concatenate([jnp.where(row < dqk, qt, zero), jnp.where(row >= dqk, qt, zero)], axis=1)

        def scores(t):
            if t == 0:
                return jnp.dot(mk_ref[...], w, preferred_element_type=F32) + mbias_ref[...]
            s = jnp.dot(kf_ref[0, (t - 1) * tq:t * tq, :], w, preferred_element_type=F32)
            return s + dbias_ref[...] if t == n - 1 else s

        s = scores(0)
        for t in range(n):
            s_next = scores(t + 1) if t + 1 < n else None
            s_max = jnp.max(s, axis=0, keepdims=True)
            if t == 0:
                m = s_max
                p = jnp.concatenate([jnp.exp2(s - m).astype(BF16), meta_fill], axis=0)
                acc = jnp.dot(jnp.concatenate([mvt_ref[...], ones[:, :LANES]], axis=0), p,
                              preferred_element_type=F32)
            else:
                m_new = jnp.maximum(m, s_max)
                acc = jnp.exp2(m - m_new) * acc + jnp.dot(
                    jnp.concatenate([vt_ref[0, t - 1], ones], axis=0), jnp.exp2(s - m_new).astype(BF16),
                    preferred_element_type=F32)
                m = m_new
            s = s_next

        l = acc[dv:dv + 1]
        ot = acc[:dv, :tq] / l[:, :tq] - lam * (acc[:dv, tq:] / l[:, tq:])
        ot = ot * lax.rsqrt(jnp.mean(ot * ot, axis=0, keepdims=True) + RMS_EPS)
        o_ref[0, q * tq:(q + 1) * tq, :] = (ot.T * g_ref[...] * (1.0 - lam_init)).astype(o_ref.dtype)

    for q in range(n_tiles):
        query_tile(q)


def _attention_bias(tq, n_meta, meta_rows):
    r = jnp.arange(tq, dtype=jnp.int32)[:, None]
    c = jnp.arange(2 * tq, dtype=jnp.int32)[None, :] % tq
    meta = jnp.broadcast_to(r[:meta_rows] < n_meta, (meta_rows, 2 * tq))
    diag = (r // CHUNK) <= (c // CHUNK)
    neg = jnp.float32(-jnp.inf)
    return jnp.where(meta, 0.0, neg), jnp.where(diag, 0.0, neg)


def _attention_t(lqk, subln_g, qt, mk, mvt, kf, vt, *, n_heads, n_meta, lam_init):
    nb, n_tiles, d, tq = qt.shape
    length = n_tiles * tq
    dv = d // n_heads
    dqk = lqk.shape[1]
    meta_rows = mk.shape[0]
    assert dv == 2 * dqk and kf.shape == (nb, length, d) and vt.shape == qt.shape
    assert mk.shape[1] == d and mvt.shape == (d, LANES) and n_meta <= meta_rows <= LANES and tq % CHUNK == 0
    return pl.pallas_call(
        functools.partial(_attn_t_kernel, tq=tq, n_tiles=n_tiles, lam_init=lam_init, dqk=dqk),
        grid=(nb, n_heads),
        in_specs=[pl.BlockSpec((4, dqk), lambda b, h: (0, 0)),
                  pl.BlockSpec((1, dv), lambda b, h: (0, 0)),
                  _const_spec((meta_rows, 2 * tq)), _const_spec((tq, 2 * tq)),
                  pl.BlockSpec((1, n_tiles, dv, tq), lambda b, h: (b, 0, h, 0)),
                  pl.BlockSpec((meta_rows, dv), lambda b, h: (0, h)),
                  pl.BlockSpec((dv, LANES), lambda b, h: (h, 0)),
                  pl.BlockSpec((1, length, dv), lambda b, h: (b, 0, h)),
                  pl.BlockSpec((1, n_tiles, dv, tq), lambda b, h: (b, 0, h, 0))],
        out_specs=pl.BlockSpec((1, length, dv), lambda b, h: (b, 0, h)),
        out_shape=jax.ShapeDtypeStruct((nb, length, d), BF16),
        compiler_params=pltpu.CompilerParams(
            dimension_semantics=("parallel", "parallel"), vmem_limit_bytes=VMEM_LIMIT),
        name="diff_attention_t",
    )(lqk, subln_g, *_attention_bias(tq, n_meta, meta_rows), qt, mk, mvt, kf, vt)


def _tail_kernel(x_ref, mc_ref, sga_ref, ya_ref, wb1_ref, wout_ref, g_ref, b_ref, wrt_ref, brt_ref,
                 h32_ref, route_ref, counts_ref, *, alpha, n_experts, n_groups, n_sub):
    @pl.when(pl.program_id(0) == 0)
    def _():
        counts_ref[...] = jnp.zeros_like(counts_ref)

    tm = x_ref.shape[0]
    sub = tm // n_sub
    lane = lax.broadcasted_iota(jnp.int32, (sub, ROUTE_LANES), 1)
    big = jnp.int32(ROUTE_LANES)
    neg = -jnp.inf
    per_group = n_experts // n_groups
    r = lax.broadcasted_iota(jnp.int32, (sub, sub), 0)
    c = lax.broadcasted_iota(jnp.int32, (sub, sub), 1)
    earlier = jnp.where(c < r, 1.0, 0.0).astype(BF16)

    def first_max(v):
        vmax = jnp.max(v, axis=-1, keepdims=True)
        return vmax, jnp.min(jnp.where(v == vmax, lane, big), axis=-1, keepdims=True)

    counts = counts_ref[...]
    for g in range(n_sub):
        rows = slice(g * sub, (g + 1) * sub)
        merged = mc_ref[rows].astype(F32) + sga_ref[rows].astype(F32) * jnp.dot(
            ya_ref[rows], wb1_ref[...], preferred_element_type=F32)
        y = alpha * x_ref[rows] + jnp.dot(merged.astype(BF16), wout_ref[...], preferred_element_type=F32)
        h = _layer_norm(y, g_ref[...], b_ref[...])
        h32_ref[rows] = h

        h_hi = h.astype(BF16)
        h_lo = (h - h_hi.astype(F32)).astype(BF16)
        hw = jnp.dot(h_hi, wrt_ref[...], preferred_element_type=F32)
        logits = (hw[:, :ROUTE_LANES] + hw[:, ROUTE_LANES:]
                  + jnp.dot(h_lo, wrt_ref[:, :ROUTE_LANES], preferred_element_type=F32)) + brt_ref[...]

        is_group = (lane >= n_experts) & (lane < n_experts + n_groups)
        lg = jnp.where(is_group, logits, neg)
        eg = jnp.exp(lg - jnp.max(lg, axis=-1, keepdims=True))
        p_group = jnp.where(is_group, eg / jnp.sum(eg, axis=-1, keepdims=True), neg)
        p_g, g_lane = first_max(p_group)
        g_idx = g_lane - n_experts
        lo = g_idx * per_group
        le = jnp.where((lane >= lo) & (lane < lo + per_group), logits, neg)
        v1, i1 = first_max(le)
        v2, i2 = first_max(jnp.where(lane == i1, neg, le))
        e2 = jnp.exp(v2 - v1)
        w1 = p_g / (1.0 + e2)
        w2 = p_g * e2 / (1.0 + e2)

        onehot = jnp.where((lane == i1) | (lane == i2), 1.0, 0.0).astype(BF16)
        before = jnp.dot(earlier, onehot, preferred_element_type=F32) + counts
        r1 = jnp.sum(jnp.where(lane == i1, before, 0.0), axis=-1, keepdims=True)
        r2 = jnp.sum(jnp.where(lane == i2, before, 0.0), axis=-1, keepdims=True)
        counts = counts + jnp.sum(onehot.astype(F32), axis=0, keepdims=True)

        route = jnp.zeros_like(logits)
        for k, col in ((ROUTE_E0, i1.astype(F32)), (ROUTE_E1, i2.astype(F32)), (ROUTE_W0, w1), (ROUTE_W1, w2),
                       (ROUTE_R0, r1), (ROUTE_R1, r2)):
            route = jnp.where(lane == k, col, route)
        route_ref[rows] = route
    counts_ref[...] = counts


def _tail(x, mc, sga, ya, wb1, w_out, ln_g, ln_b, w_rt, b_rt, *, tm, alpha, n_experts, n_groups):
    t, d = x.shape
    assert t % tm == 0
    n_sub = 2 if tm % (2 * LANES) == 0 else 1
    row = lambda w: pl.BlockSpec((tm, w), lambda i: (i, 0))
    return pl.pallas_call(
        functools.partial(_tail_kernel, alpha=alpha, n_experts=n_experts, n_groups=n_groups, n_sub=n_sub),
        grid=(t // tm,),
        in_specs=[row(d), row(d), row(d), row(d), _const_spec((d, d)), _const_spec((d, d)),
                  _const_spec((1, d)), _const_spec((1, d)),
                  _const_spec((d, 2 * ROUTE_LANES)), _const_spec((1, ROUTE_LANES))],
        out_specs=[row(d), row(ROUTE_LANES), pl.BlockSpec((1, ROUTE_LANES), lambda i: (0, 0))],
        out_shape=[jax.ShapeDtypeStruct((t, d), F32), jax.ShapeDtypeStruct((t, ROUTE_LANES), F32),
                   jax.ShapeDtypeStruct((1, ROUTE_LANES), F32)],
        compiler_params=pltpu.CompilerParams(
            dimension_semantics=("arbitrary",), vmem_limit_bytes=VMEM_LIMIT),
        name="merge_ln_router",
    )(x, mc, sga, ya, wb1, w_out, ln_g, ln_b, w_rt, b_rt)


def _row_copy(src_ref, src_row, dst_ref, dst_row, sem):
    return pltpu.make_async_copy(src_ref.at[pl.ds(src_row, 1)], dst_ref.at[pl.ds(dst_row, 1)], sem)


def _moe_scatter_kernel(pos_ref, pad_rows_ref, n_pad_ref, n_used_ref, h_ref, xs_ref, zero_ref, sem, pad_sem,
                        *, tm):
    @pl.when(pl.program_id(0) == 0)
    def _():
        zero_ref[...] = jnp.zeros_like(zero_ref)
        tr = zero_ref.shape[0]
        n_tiles = xs_ref.shape[0] // tr

        def tile_copy(r):
            return pltpu.make_async_copy(zero_ref, xs_ref.at[pl.ds(pl.multiple_of(r * tr, tr), tr)], pad_sem)

        def issue_pad(j, _):
            _row_copy(zero_ref, 0, xs_ref, pad_rows_ref[j], pad_sem).start()
            return 0

        def wait_pad(j, _):
            _row_copy(zero_ref, 0, xs_ref, 0, pad_sem).wait()
            return 0

        lax.fori_loop(0, n_pad_ref[0], issue_pad, 0)
        lax.fori_loop(n_used_ref[0], n_tiles, lambda r, _: (tile_copy(r).start(), 0)[1], 0)
        lax.fori_loop(0, n_pad_ref[0], wait_pad, 0)
        lax.fori_loop(n_used_ref[0], n_tiles, lambda r, _: (tile_copy(r).wait(), 0)[1], 0)

    def issue(t, _):
        for k in range(TOP_K):
            _row_copy(h_ref, t, xs_ref, pos_ref[0, 0, k * tm + t], sem).start()
        return 0

    lax.fori_loop(0, tm, issue, 0)
    for _ in range(TOP_K):
        pltpu.make_async_copy(h_ref, xs_ref.at[pl.ds(0, tm)], sem).wait()


def _moe_scatter(pos, pad_rows, n_pad, n_used, h32, n_rows, *, tm, tr):
    t, w = h32.shape
    smem = pl.BlockSpec(memory_space=pltpu.SMEM)
    return pl.pallas_call(
        functools.partial(_moe_scatter_kernel, tm=tm),
        grid=(t // tm,),
        in_specs=[pl.BlockSpec((1, 1, TOP_K * tm), lambda i: (i, 0, 0), memory_space=pltpu.SMEM),
                  smem, smem, smem, pl.BlockSpec((tm, w), lambda i: (i, 0))],
        out_specs=pl.BlockSpec(memory_space=pl.ANY),
        out_shape=jax.ShapeDtypeStruct((n_rows, w), h32.dtype),
        scratch_shapes=[pltpu.VMEM((tr, w), h32.dtype), pltpu.SemaphoreType.DMA(()), pltpu.SemaphoreType.DMA(())],
        compiler_params=pltpu.CompilerParams(
            dimension_semantics=("arbitrary",), vmem_limit_bytes=VMEM_LIMIT, has_side_effects=True),
        name="moe_scatter",
    )(pos, pad_rows, n_pad, n_used, h32)


def _moe_experts_kernel(tile_expert_ref, n_used_ref, xs_ref, wgu_ref, wd_ref, ys_ref, *, d_expert):
    del tile_expert_ref
    used = pl.program_id(0) < n_used_ref[0]

    @pl.when(used)
    def _():
        ab = jnp.dot(xs_ref[...].astype(BF16), wgu_ref[0], preferred_element_type=F32)
        a = ab[:, :d_expert]
        act = (a * jax.nn.sigmoid(a) * ab[:, d_expert:]).astype(BF16)
        ys_ref[...] = jnp.dot(act, wd_ref[0], preferred_element_type=F32)

    @pl.when(jnp.logical_not(used))
    def _():
        ys_ref[...] = jnp.zeros_like(ys_ref)


def _moe_experts(tile_expert, n_used, xs, wgu, wd, *, tr):
    n_rows, w = xs.shape
    _, d, two_de = wgu.shape
    assert d == w
    last_used = lambda r, nu: jnp.minimum(r, nu[0] - 1)
    return pl.pallas_call(
        functools.partial(_moe_experts_kernel, d_expert=two_de // 2),
        grid_spec=pltpu.PrefetchScalarGridSpec(
            num_scalar_prefetch=2,
            grid=(n_rows // tr,),
            in_specs=[pl.BlockSpec((tr, w), lambda r, te, nu: (last_used(r, nu), 0)),
                      pl.BlockSpec((1, d, two_de), lambda r, te, nu: (te[r], 0, 0)),
                      pl.BlockSpec((1, two_de // 2, d), lambda r, te, nu: (te[r], 0, 0))],
            out_specs=pl.BlockSpec((tr, w), lambda r, te, nu: (r, 0))),
        out_shape=jax.ShapeDtypeStruct((n_rows, w), xs.dtype),
        compiler_params=pltpu.CompilerParams(
            dimension_semantics=("arbitrary",), vmem_limit_bytes=VMEM_LIMIT),
        name="moe_experts",
    )(tile_expert, n_used, xs, wgu, wd)


def _moe_combine_kernel(pos_ref, route_ref, h_ref, g_ref, b_ref, ys_ref, o_ref, y_buf, sem, *, tm, alpha):
    def issue(t, _):
        for k in range(TOP_K):
            _row_copy(ys_ref, pos_ref[0, 0, k * tm + t], y_buf.at[k], t, sem).start()
        return 0

    lax.fori_loop(0, tm, issue, 0)
    for k in range(TOP_K):
        pltpu.make_async_copy(ys_ref.at[pl.ds(0, tm)], y_buf.at[k], sem).wait()

    route = route_ref[...]
    lane = lax.broadcasted_iota(jnp.int32, route.shape, 1)
    f = jnp.zeros_like(h_ref)
    for k, w_lane in enumerate((ROUTE_W0, ROUTE_W1)):
        gate = jnp.sum(jnp.where(lane == w_lane, route, 0.0), axis=-1, keepdims=True)
        f = f + gate * y_buf[k]
    o_ref[...] = _layer_norm(alpha * h_ref[...] + f, g_ref[...], b_ref[...])


def _moe_combine(pos, route, h32, ys, ln_g, ln_b, *, tm, alpha):
    t, d = h32.shape
    return pl.pallas_call(
        functools.partial(_moe_combine_kernel, tm=tm, alpha=alpha),
        grid=(t // tm,),
        in_specs=[pl.BlockSpec((1, 1, TOP_K * tm), lambda i: (i, 0, 0), memory_space=pltpu.SMEM),
                  pl.BlockSpec((tm, ROUTE_LANES), lambda i: (i, 0)),
                  pl.BlockSpec((tm, d), lambda i: (i, 0)),
                  _const_spec((1, d)), _const_spec((1, d)),
                  pl.BlockSpec(memory_space=pl.ANY)],
        out_specs=pl.BlockSpec((tm, d), lambda i: (i, 0)),
        out_shape=jax.ShapeDtypeStruct((t, d), F32),
        scratch_shapes=[pltpu.VMEM((TOP_K, tm) + ys.shape[1:], ys.dtype), pltpu.SemaphoreType.DMA(())],
        compiler_params=pltpu.CompilerParams(
            dimension_semantics=("arbitrary",), vmem_limit_bytes=VMEM_LIMIT),
        name="moe_combine",
    )(pos, route, h32, ln_g, ln_b, ys)


def _moe(h32, route, counts, wgu, wd, ln_g, ln_b, *, tm, alpha):
    t, d = h32.shape
    n_experts = wgu.shape[0]
    tr = min(MOE_ROW_TILE, max(LANES, TOP_K * t // n_experts))
    assert t % tm == 0
    n_tiles = (TOP_K * t + n_experts * (tr - 1)) // tr

    def segment_of(i, seg_ends):
        return jnp.minimum(jnp.sum(i[:, None] >= seg_ends[None, :], axis=1), n_experts - 1).astype(jnp.int32)

    experts = jnp.arange(n_experts, dtype=jnp.int32)
    cnt = counts[0, :n_experts].astype(jnp.int32)
    padded = (cnt + tr - 1) // tr * tr
    ends = jnp.cumsum(padded)
    starts = ends - padded
    expert = route[:, ROUTE_E0:ROUTE_E1 + 1].astype(jnp.int32)
    rank = route[:, ROUTE_R0:ROUTE_R1 + 1].astype(jnp.int32)
    pos = jnp.sum(jnp.where(expert[..., None] == experts, starts, 0), axis=-1) + rank
    pos = pos.reshape(t // tm, tm, TOP_K).transpose(0, 2, 1).reshape(t // tm, 1, TOP_K * tm)
    tile_expert = segment_of(jnp.arange(n_tiles, dtype=jnp.int32) * tr, ends)
    n_used = ends[-1:] // tr
    pad_ends = jnp.cumsum(padded - cnt)
    i = jnp.arange(n_experts * (tr - 1), dtype=jnp.int32)
    seg = segment_of(i, pad_ends)
    pad_rows = jnp.minimum((starts + cnt)[seg] + i - (pad_ends - (padded - cnt))[seg], n_tiles * tr - 1)

    xs = _moe_scatter(pos, pad_rows, pad_ends[-1:], n_used, h32, n_tiles * tr, tm=tm, tr=tr)
    ys = _moe_experts(tile_expert, n_used, xs, wgu, wd, tr=tr)
    return _moe_combine(pos, route, h32, ys, ln_g, ln_b, tm=tm, alpha=alpha)


def _pick_tile(n, target):
    t = min(n, target)
    while n % t:
        t //= 2
    return t


def kernel(x_prompt, x_sample, cache_k, cache_v, state_conv, meta_tokens, w_in, conv_w, lambda_qk, subln_g, w_branch, w_out, ln1_g, ln1_b, w_group, b_group, w_router, b_router, w_gate_up, w_down, ln2_g, ln2_b):
    depth = w_in.shape[0]
    assert depth == 1, "single-layer step only"
    bp, seq, d = x_prompt.shape
    bs, s_len, _ = x_sample.shape
    n_meta = meta_tokens.shape[0]
    n_heads = cache_k.shape[3]
    dqk = cache_k.shape[4] // 2
    past = cache_k.shape[2] - n_meta
    n_groups = w_group.shape[-1]
    n_experts = w_router.shape[-1]
    assert n_experts + n_groups <= ROUTE_LANES
    q_scale = dqk ** -0.5
    assert math.frexp(q_scale)[0] == 0.5, "the score scale is folded into q; exact only for powers of two"
    alpha = (2.0 * depth) ** 0.25
    lam_init = 0.8 - 0.6 * math.exp(-0.3 * 0)

    w_in_b = w_in[0].astype(BF16)
    wb = w_branch[0].astype(BF16)
    w_out_b = w_out[0].astype(BF16)
    wgu_b = w_gate_up[0].astype(BF16)
    wd_b = w_down[0].astype(BF16)
    w_rt = jnp.zeros((d, ROUTE_LANES), F32).at[:, :n_experts].set(w_router[0]).at[
        :, n_experts:n_experts + n_groups].set(w_group[0])
    w_rt_hi = w_rt.astype(BF16)
    w_rt = jnp.concatenate([w_rt_hi, (w_rt - w_rt_hi.astype(F32)).astype(BF16)], axis=1)
    b_rt = jnp.zeros((1, ROUTE_LANES), F32).at[0, :n_experts].set(b_router[0]).at[
        0, n_experts:n_experts + n_groups].set(b_group[0])

    def hist_rows(rows):
        return jnp.pad(rows, ((0, 0), (HIST - rows.shape[1], 0), (0, 0)))

    proj = functools.partial(_proj_conv, w_in=w_in_b, conv_w=conv_w[0], wb0=wb[0], wqt=w_in_b[:, 3 * d:4 * d].T,
                             q_scale=q_scale)
    attn = functools.partial(_attention, lambda_qk[0], subln_g, n_heads=n_heads, lam_init=lam_init)
    tail = functools.partial(_tail, wb1=wb[1], w_out=w_out_b, ln_g=ln1_g, ln_b=ln1_b, w_rt=w_rt, b_rt=b_rt,
                             alpha=alpha, n_experts=n_experts, n_groups=n_groups)
    moe = functools.partial(_moe, wgu=wgu_b, wd=wd_b, ln_g=ln2_g, ln_b=ln2_b, alpha=alpha)

    mk32, mv32, _, mkb, mvb, _, _, mnc = proj(
        meta_tokens[None], jnp.zeros((1, HIST, d), F32), bb=1, ts=n_meta, transposed=False)
    ta = _pick_tile(seq, ATTN_TILE)
    assert n_meta <= LANES
    mk_pad = jnp.pad(mkb[0], ((0, -n_meta % BF16_SUBLANES), (0, 0)))
    mvt_pad = jnp.pad(mvb[0].T, ((0, 0), (0, LANES - n_meta)))
    k32, v32, qt, kb, vt, mc, sga, nc = proj(
        x_prompt, jnp.broadcast_to(mnc, (bp, HIST, d)), bb=1, ts=_pick_tile(seq, PROJ_TILE), transposed=True, ta=ta)
    ya = _attention_t(lambda_qk[0], subln_g, qt, mk_pad, mvt_pad, kb, vt,
                      n_heads=n_heads, n_meta=n_meta, lam_init=lam_init)
    t_p = bp * seq
    tm = _pick_tile(t_p, 512)
    routed = tail(x_prompt.reshape(t_p, d), mc.reshape(t_p, d), sga.reshape(t_p, d), ya.reshape(t_p, d), tm=tm)
    y_prompt = moe(*routed, tm=tm).reshape(bp, seq, d)

    sk32, sv32, sqb, skb, svb, smc, ssga, snc = proj(
        x_sample, hist_rows(state_conv[0]), bb=bs, ts=s_len, transposed=False)
    sya = attn(sqb, cache_k[0].reshape(bs, n_meta + past, d), cache_v[0].reshape(bs, n_meta + past, d),
               skb, svb, tq=s_len, frame0=past)
    t_s = bs * s_len
    srouted = tail(x_sample.reshape(t_s, d), smc.reshape(t_s, d), ssga.reshape(t_s, d), sya.reshape(t_s, d),
                   tm=t_s)
    y_sample = moe(*srouted, tm=t_s).reshape(bs, s_len, d)

    def with_meta(m, f):
        full = jnp.concatenate([jnp.broadcast_to(m, (bp, n_meta, d)), f], axis=1)
        return full.reshape(1, bp, n_meta + seq, n_heads, d // n_heads)

    return (y_prompt, y_sample,
            with_meta(mk32, k32), with_meta(mv32, v32), nc[None, :, HIST - 2:],
            sk32.reshape(1, bs, s_len, n_heads, d // n_heads),
            sv32.reshape(1, bs, s_len, n_heads, d // n_heads), snc[None, :, HIST - 2:])
```

```python
import functools
import math

import jax
import jax.numpy as jnp
from jax import lax
from jax.experimental import pallas as pl
from jax.experimental.pallas import tpu as pltpu

CHUNK = 64
CHUNK_SHIFT = CHUNK.bit_length() - 1
assert 1 << CHUNK_SHIFT == CHUNK
LANES = 128
BF16_SUBLANES = 16
LOG2_E = math.log2(math.e)
LN_EPS = 1e-5
RMS_EPS = 1e-5
HIST = 8
ROUTE_LANES = 128
ROUTE_E0, ROUTE_E1, ROUTE_W0, ROUTE_W1, ROUTE_R0, ROUTE_R1 = range(6)
TOP_K = 2
ROW_DMA_UNROLL = 8
MOE_ROW_TILE = 512
PROJ_TILE = 512
ATTN_TILE = 256
VMEM_LIMIT = 52 * 1024 * 1024

F32 = jnp.float32
BF16 = jnp.bfloat16


def _const_spec(shape):
    return pl.BlockSpec(shape, lambda *_: (0,) * len(shape), pipeline_mode=pl.Buffered(1))


def _layer_norm(x, g, b):
    mu = jnp.mean(x, axis=-1, keepdims=True)
    xc = x - mu
    var = jnp.mean(xc * xc, axis=-1, keepdims=True)
    return xc * lax.rsqrt(var + LN_EPS) * g + b


def _proj_conv_kernel(x_ref, c0_ref, win_ref, cw_ref, wb0_ref, wqt_ref,
                      k32_ref, v32_ref, q_ref, kb_ref, vb_ref, mc_ref, sga_ref, nc_ref,
                      carry_ref, *, bb, ts, d, q_scale, transposed):
    @pl.when(pl.program_id(1) == 0)
    def _():
        carry_ref[...] = c0_ref[...]

    xb = x_ref[...].reshape(bb * ts, d).astype(BF16)

    def proj(i):
        return jnp.dot(xb, win_ref[:, i * d:(i + 1) * d], preferred_element_type=F32)

    u = proj(2) * proj(0)
    cw = cw_ref[...]
    row = lax.broadcasted_iota(jnp.int32, (ts, d), 0)
    convs = []
    for b in range(bb):
        ub = u[b * ts:(b + 1) * ts]
        hist = carry_ref[b]
        h1 = hist[HIST - 1:HIST]
        h2 = hist[HIST - 2:HIST - 1]
        um1 = jnp.where(row == 0, h1, pltpu.roll(ub, 1, 0))
        um2 = jnp.where(row == 0, h2, jnp.where(row == 1, h1, pltpu.roll(ub, 2, 0)))
        convs.append(cw[0:1] * um2 + cw[1:2] * um1 + cw[2:3] * ub)
        carry_ref[b] = ub[ts - HIST:ts]
        nc_ref[b] = ub[ts - HIST:ts]
    conv = convs[0] if bb == 1 else jnp.concatenate(convs, axis=0)
    yc = (proj(1) * conv).astype(BF16)
    mc = jax.nn.sigmoid(proj(6)) * jnp.dot(yc, wb0_ref[...], preferred_element_type=F32)
    mc_ref[...] = mc.astype(BF16).reshape(bb, ts, d)
    sga_ref[...] = jax.nn.sigmoid(proj(7)).astype(BF16).reshape(bb, ts, d)
    k = proj(4)
    k32_ref[...] = k.reshape(bb, ts, d)
    kb_ref[...] = k.astype(BF16).reshape(bb, ts, d)
    v = proj(5)
    v32_ref[...] = v.reshape(bb, ts, d)
    if transposed:
        nt = (((1,), (1,)), ((), ()))
        qt = lax.dot_general(wqt_ref[...], xb, nt, preferred_element_type=F32)
        qt = (qt * (q_scale * LOG2_E)).astype(BF16)
        vt = v.T.astype(BF16)
        ta = q_ref.shape[3]
        for j in range(ts // ta):
            q_ref[0, j] = qt[:, j * ta:(j + 1) * ta]
            vb_ref[0, j] = vt[:, j * ta:(j + 1) * ta]
    else:
        q_ref[...] = (proj(3) * q_scale).astype(BF16).reshape(bb, ts, d)
        vb_ref[...] = v.astype(BF16).reshape(bb, ts, d)


def _proj_conv(x, c0, w_in, conv_w, wb0, wqt, *, bb, ts, q_scale, transposed, ta=None):
    nb, length, d = x.shape
    assert nb % bb == 0 and length % ts == 0 and ts % HIST == 0
    assert w_in.shape == (d, 8 * d), "all eight projection sections must be d_model wide"
    assert not transposed or (bb == 1 and ts % ta == 0)
    blk = pl.BlockSpec((bb, ts, d), lambda b, s: (b, s, 0))
    hist_blk = pl.BlockSpec((bb, HIST, d), lambda b, s: (b, 0, 0))
    f32_out = jax.ShapeDtypeStruct((nb, length, d), F32)
    bf_out = jax.ShapeDtypeStruct((nb, length, d), BF16)
    if transposed:
        qv_blk = pl.BlockSpec((1, ts // ta, d, ta), lambda b, s: (b, s, 0, 0))
        qv_out = jax.ShapeDtypeStruct((nb, length // ta, d, ta), BF16)
    else:
        qv_blk, qv_out = blk, bf_out
    return pl.pallas_call(
        functools.partial(_proj_conv_kernel, bb=bb, ts=ts, d=d, q_scale=q_scale, transposed=transposed),
        grid=(nb // bb, length // ts),
        in_specs=[blk, hist_blk, _const_spec((d, 8 * d)), _const_spec((3, d)), _const_spec((d, d)),
                  _const_spec((d, d))],
        out_specs=[blk, blk, qv_blk, blk, qv_blk, blk, blk, hist_blk],
        out_shape=[f32_out, f32_out, qv_out, bf_out, qv_out, bf_out, bf_out,
                   jax.ShapeDtypeStruct((nb, HIST, d), F32)],
        scratch_shapes=[pltpu.VMEM((bb, HIST, d), F32)],
        compiler_params=pltpu.CompilerParams(
            dimension_semantics=("parallel", "arbitrary"), vmem_limit_bytes=VMEM_LIMIT),
        name="proj_conv",
    )(x, c0, w_in, conv_w, wb0, wqt)


def _pad_keys(x):
    pad = -x.shape[0] % LANES
    return x if pad == 0 else jnp.concatenate([x, jnp.zeros((pad, x.shape[1]), x.dtype)], axis=0)


def _attn_kernel(lqk_ref, g_ref, q_ref, ck_ref, cv_ref, kf_ref, vf_ref, o_ref,
                 *, tq, frame0, lam_init, dqk):
    qi = pl.program_id(2)
    lqk = lqk_ref[...]
    lam = (jnp.exp(jnp.sum(lqk[0:1] * lqk[1:2], axis=-1, keepdims=True))
           - jnp.exp(jnp.sum(lqk[2:3] * lqk[3:4], axis=-1, keepdims=True)) + lam_init)

    q = q_ref[0]
    lane = lax.broadcasted_iota(jnp.int32, q.shape, 1)
    zero = jnp.zeros_like(q)
    qs = jnp.concatenate([jnp.where(lane < dqk, q, zero), jnp.where(lane >= dqk, q, zero)], axis=0)

    def scores(k):
        return lax.dot_general(qs, k, (((1,), (1,)), ((), ())), preferred_element_type=F32)

    n_ctx = ck_ref.shape[1]
    s = scores(_pad_keys(ck_ref[0].astype(BF16)))
    if n_ctx % LANES:
        s = jnp.where(lax.broadcasted_iota(jnp.int32, s.shape, 1) < n_ctx, s, -jnp.inf)
    m = jnp.max(s, axis=-1, keepdims=True)
    p = jnp.exp(s - m)
    l = jnp.sum(p, axis=-1, keepdims=True)
    acc = jnp.dot(p.astype(BF16), _pad_keys(cv_ref[0].astype(BF16)), preferred_element_type=F32)

    def step(j, carry, masked):
        m, l, acc = carry
        start = pl.multiple_of(j * tq, tq)
        s = scores(_pad_keys(kf_ref[0, pl.ds(start, tq), :]))
        if masked:
            r = lax.broadcasted_iota(jnp.int32, s.shape, 0)
            c = lax.broadcasted_iota(jnp.int32, s.shape, 1)
            q_chunk = lax.shift_right_logical(frame0 + j * tq + jnp.where(r >= tq, r - tq, r), CHUNK_SHIFT)
            k_chunk = lax.shift_right_logical(frame0 + j * tq + c, CHUNK_SHIFT)
            s = jnp.where((k_chunk <= q_chunk) & (c < tq), s, -jnp.inf)
        m_new = jnp.maximum(m, jnp.max(s, axis=-1, keepdims=True))
        alpha = jnp.exp(m - m_new)
        p = jnp.exp(s - m_new)
        l = alpha * l + jnp.sum(p, axis=-1, keepdims=True)
        acc = alpha * acc + jnp.dot(p.astype(BF16), _pad_keys(vf_ref[0, pl.ds(start, tq), :]),
                                    preferred_element_type=F32)
        return m_new, l, acc

    carry = lax.fori_loop(0, qi, functools.partial(step, masked=False), (m, l, acc))
    m, l, acc = step(qi, carry, masked=True)

    o = acc[:tq] / l[:tq] - lam * (acc[tq:] / l[tq:])
    o = o * lax.rsqrt(jnp.mean(o * o, axis=-1, keepdims=True) + RMS_EPS) * g_ref[...] * (1.0 - lam_init)
    o_ref[0] = o.astype(o_ref.dtype)


def _attention(lqk, subln_g, q, ctx_k, ctx_v, kf, vf, *, n_heads, tq, frame0, lam_init):
    nb, length, d = q.shape
    dv = d // n_heads
    dqk = lqk.shape[1]
    assert dv == 2 * dqk and length % tq == 0
    assert tq % LANES == 0 or length == tq, "only a single (diagonal) frame tile may be lane-padded"
    n_ctx = ctx_k.shape[1]
    assert ctx_k.shape == ctx_v.shape == (nb, n_ctx, d)
    q_blk = pl.BlockSpec((1, tq, dv), lambda b, h, i: (b, i, h))
    ctx_blk = pl.BlockSpec((1, n_ctx, dv), lambda b, h, i: (b, 0, h))
    kv_blk = pl.BlockSpec((1, length, dv), lambda b, h, i: (b, 0, h))
    return pl.pallas_call(
        functools.partial(_attn_kernel, tq=tq, frame0=frame0, lam_init=lam_init, dqk=dqk),
        grid=(nb, n_heads, length // tq),
        in_specs=[pl.BlockSpec((4, dqk), lambda b, h, i: (0, 0)),
                  pl.BlockSpec((1, dv), lambda b, h, i: (0, 0)),
                  q_blk, ctx_blk, ctx_blk, kv_blk, kv_blk],
        out_specs=q_blk,
        out_shape=jax.ShapeDtypeStruct((nb, length, d), BF16),
        compiler_params=pltpu.CompilerParams(
            dimension_semantics=("parallel", "parallel", "arbitrary"), vmem_limit_bytes=VMEM_LIMIT),
        name="diff_attention",
    )(lqk, subln_g, q, ctx_k, ctx_v, kf, vf)


def _attn_t_kernel(lqk_ref, g_ref, mbias_ref, dbias_ref, qt_ref, mk_ref, mvt_ref, kf_ref, vt_ref, o_ref,
                   *, tq, n_tiles, lam_init, dqk):
    lqk = lqk_ref[...]
    lam = (jnp.exp(jnp.sum(lqk[0:1] * lqk[1:2], axis=-1, keepdims=True))
           - jnp.exp(jnp.sum(lqk[2:3] * lqk[3:4], axis=-1, keepdims=True)) + lam_init)

    dv = qt_ref.shape[2]
    row = lax.broadcasted_iota(jnp.int32, (dv, tq), 0)
    zero = jnp.zeros((dv, tq), BF16)
    ones = jnp.ones((BF16_SUBLANES, tq), BF16)
    meta_fill = jnp.zeros((LANES - mk_ref.shape[0], 2 * tq), BF16)

    def query_tile(q):
        n = q + 2
        qt = qt_ref[0, q]
        w = jnp.concatenate([jnp.where(row < dqk, qt, zero), jnp.where(row >= dqk, qt, zero)], axis=1)

        def scores(t):
            if t == 0:
                return jnp.dot(mk_ref[...], w, preferred_element_type=F32) + mbias_ref[...]
            s = jnp.dot(kf_ref[0, (t - 1) * tq:t * tq, :], w, preferred_element_type=F32)
            return s + dbias_ref[...] if t == n - 1 else s

        s = scores(0)
        for t in range(n):
            s_next = scores(t + 1) if t + 1 < n else None
            s_max = jnp.max(s, axis=0, keepdims=True)
            if t == 0:
                m = s_max
                p = jnp.concatenate([jnp.exp2(s - m).astype(BF16), meta_fill], axis=0)
                acc = jnp.dot(jnp.concatenate([mvt_ref[...], ones[:, :LANES]], axis=0), p,
                              preferred_element_type=F32)
            else:
                m_new = jnp.maximum(m, s_max)
                acc = jnp.exp2(m - m_new) * acc + jnp.dot(
                    jnp.concatenate([vt_ref[0, t - 1], ones], axis=0), jnp.exp2(s - m_new).astype(BF16),
                    preferred_element_type=F32)
                m = m_new
            s = s_next

        l = acc[dv:dv + 1]
        ot = acc[:dv, :tq] / l[:, :tq] - lam * (acc[:dv, tq:] / l[:, tq:])
        ot = ot * lax.rsqrt(jnp.mean(ot * ot, axis=0, keepdims=True) + RMS_EPS)
        o_ref[0, q * tq:(q + 1) * tq, :] = (ot.T * g_ref[...] * (1.0 - lam_init)).astype(o_ref.dtype)

    for q in range(n_tiles):
        query_tile(q)


def _attention_bias(tq, n_meta, meta_rows):
    r = jnp.arange(tq, dtype=jnp.int32)[:, None]
    c = jnp.arange(2 * tq, dtype=jnp.int32)[None, :] % tq
    meta = jnp.broadcast_to(r[:meta_rows] < n_meta, (meta_rows, 2 * tq))
    diag = (r // CHUNK) <= (c // CHUNK)
    neg = jnp.float32(-jnp.inf)
    return jnp.where(meta, 0.0, neg), jnp.where(diag, 0.0, neg)


def _attention_t(lqk, subln_g, qt, mk, mvt, kf, vt, *, n_heads, n_meta, lam_init):
    nb, n_tiles, d, tq = qt.shape
    length = n_tiles * tq
    dv = d // n_heads
    dqk = lqk.shape[1]
    meta_rows = mk.shape[0]
    assert dv == 2 * dqk and kf.shape == (nb, length, d) and vt.shape == qt.shape
    assert mk.shape[1] == d and mvt.shape == (d, LANES) and n_meta <= meta_rows <= LANES and tq % CHUNK == 0
    return pl.pallas_call(
        functools.partial(_attn_t_kernel, tq=tq, n_tiles=n_tiles, lam_init=lam_init, dqk=dqk),
        grid=(nb, n_heads),
        in_specs=[pl.BlockSpec((4, dqk), lambda b, h: (0, 0)),
                  pl.BlockSpec((1, dv), lambda b, h: (0, 0)),
                  _const_spec((meta_rows, 2 * tq)), _const_spec((tq, 2 * tq)),
                  pl.BlockSpec((1, n_tiles, dv, tq), lambda b, h: (b, 0, h, 0)),
                  pl.BlockSpec((meta_rows, dv), lambda b, h: (0, h)),
                  pl.BlockSpec((dv, LANES), lambda b, h: (h, 0)),
                  pl.BlockSpec((1, length, dv), lambda b, h: (b, 0, h)),
                  pl.BlockSpec((1, n_tiles, dv, tq), lambda b, h: (b, 0, h, 0))],
        out_specs=pl.BlockSpec((1, length, dv), lambda b, h: (b, 0, h)),
        out_shape=jax.ShapeDtypeStruct((nb, length, d), BF16),
        compiler_params=pltpu.CompilerParams(
            dimension_semantics=("parallel", "parallel"), vmem_limit_bytes=VMEM_LIMIT),
        name="diff_attention_t",
    )(lqk, subln_g, *_attention_bias(tq, n_meta, meta_rows), qt, mk, mvt, kf, vt)


def _tail_kernel(x_ref, mc_ref, sga_ref, ya_ref, wb1_ref, wout_ref, g_ref, b_ref, wrt_ref, brt_ref,
                 h32_ref, route_ref, counts_ref, *, alpha, n_experts, n_groups, n_sub):
    @pl.when(pl.program_id(0) == 0)
    def _():
        counts_ref[...] = jnp.zeros_like(counts_ref)

    tm = x_ref.shape[0]
    sub = tm // n_sub
    lane = lax.broadcasted_iota(jnp.int32, (sub, ROUTE_LANES), 1)
    big = jnp.int32(ROUTE_LANES)
    neg = -jnp.inf
    per_group = n_experts // n_groups
    r = lax.broadcasted_iota(jnp.int32, (sub, sub), 0)
    c = lax.broadcasted_iota(jnp.int32, (sub, sub), 1)
    earlier = jnp.where(c < r, 1.0, 0.0).astype(BF16)

    def first_max(v):
        vmax = jnp.max(v, axis=-1, keepdims=True)
        return vmax, jnp.min(jnp.where(v == vmax, lane, big), axis=-1, keepdims=True)

    counts = counts_ref[...]
    for g in range(n_sub):
        rows = slice(g * sub, (g + 1) * sub)
        merged = mc_ref[rows].astype(F32) + sga_ref[rows].astype(F32) * jnp.dot(
            ya_ref[rows], wb1_ref[...], preferred_element_type=F32)
        y = alpha * x_ref[rows] + jnp.dot(merged.astype(BF16), wout_ref[...], preferred_element_type=F32)
        h = _layer_norm(y, g_ref[...], b_ref[...])
        h32_ref[rows] = h

        h_hi = h.astype(BF16)
        h_lo = (h - h_hi.astype(F32)).astype(BF16)
        hw = jnp.dot(h_hi, wrt_ref[...], preferred_element_type=F32)
        logits = (hw[:, :ROUTE_LANES] + hw[:, ROUTE_LANES:]
                  + jnp.dot(h_lo, wrt_ref[:, :ROUTE_LANES], preferred_element_type=F32)) + brt_ref[...]

        is_group = (lane >= n_experts) & (lane < n_experts + n_groups)
        lg = jnp.where(is_group, logits, neg)
        eg = jnp.exp(lg - jnp.max(lg, axis=-1, keepdims=True))
        p_group = jnp.where(is_group, eg / jnp.sum(eg, axis=-1, keepdims=True), neg)
        p_g, g_lane = first_max(p_group)
        g_idx = g_lane - n_experts
        lo = g_idx * per_group
        le = jnp.where((lane >= lo) & (lane < lo + per_group), logits, neg)
        v1, i1 = first_max(le)
        v2, i2 = first_max(jnp.where(lane == i1, neg, le))
        e2 = jnp.exp(v2 - v1)
        w1 = p_g / (1.0 + e2)
        w2 = p_g * e2 / (1.0 + e2)

        onehot = jnp.where((lane == i1) | (lane == i2), 1.0, 0.0).astype(BF16)
        before = jnp.dot(earlier, onehot, preferred_element_type=F32) + counts
        r1 = jnp.sum(jnp.where(lane == i1, before, 0.0), axis=-1, keepdims=True)
        r2 = jnp.sum(jnp.where(lane == i2, before, 0.0), axis=-1, keepdims=True)
        counts = counts + jnp.sum(onehot.astype(F32), axis=0, keepdims=True)

        route = jnp.zeros_like(logits)
        for k, col in ((ROUTE_E0, i1.astype(F32)), (ROUTE_E1, i2.astype(F32)), (ROUTE_W0, w1), (ROUTE_W1, w2),
                       (ROUTE_R0, r1), (ROUTE_R1, r2)):
            route = jnp.where(lane == k, col, route)
        route_ref[rows] = route
    counts_ref[...] = counts


def _tail(x, mc, sga, ya, wb1, w_out, ln_g, ln_b, w_rt, b_rt, *, tm, alpha, n_experts, n_groups):
    t, d = x.shape
    assert t % tm == 0
    n_sub = 2 if tm % (2 * LANES) == 0 else 1
    row = lambda w: pl.BlockSpec((tm, w), lambda i: (i, 0))
    return pl.pallas_call(
        functools.partial(_tail_kernel, alpha=alpha, n_experts=n_experts, n_groups=n_groups, n_sub=n_sub),
        grid=(t // tm,),
        in_specs=[row(d), row(d), row(d), row(d), _const_spec((d, d)), _const_spec((d, d)),
                  _const_spec((1, d)), _const_spec((1, d)),
                  _const_spec((d, 2 * ROUTE_LANES)), _const_spec((1, ROUTE_LANES))],
        out_specs=[row(d), row(ROUTE_LANES), pl.BlockSpec((1, ROUTE_LANES), lambda i: (0, 0))],
        out_shape=[jax.ShapeDtypeStruct((t, d), F32), jax.ShapeDtypeStruct((t, ROUTE_LANES), F32),
                   jax.ShapeDtypeStruct((1, ROUTE_LANES), F32)],
        compiler_params=pltpu.CompilerParams(
            dimension_semantics=("arbitrary",), vmem_limit_bytes=VMEM_LIMIT),
        name="merge_ln_router",
    )(x, mc, sga, ya, wb1, w_out, ln_g, ln_b, w_rt, b_rt)


def _row_copy(src_ref, src_row, dst_ref, dst_row, sem):
    return pltpu.make_async_copy(src_ref.at[pl.ds(src_row, 1)], dst_ref.at[pl.ds(dst_row, 1)], sem)


def _moe_scatter_kernel(pos_ref, pad_rows_ref, n_pad_ref, n_used_ref, h_ref, xs_ref, zero_ref, sem, pad_sem,
                        *, tm):
    @pl.when(pl.program_id(0) == 0)
    def _():
        zero_ref[...] = jnp.zeros_like(zero_ref)
        tr = zero_ref.shape[0]
        n_tiles = xs_ref.shape[0] // tr

        def tile_copy(r):
            return pltpu.make_async_copy(zero_ref, xs_ref.at[pl.ds(pl.multiple_of(r * tr, tr), tr)], pad_sem)

        def issue_pad(j, _):
            _row_copy(zero_ref, 0, xs_ref, pad_rows_ref[j], pad_sem).start()
            return 0

        def wait_pad(j, _):
            _row_copy(zero_ref, 0, xs_ref, 0, pad_sem).wait()
            return 0

        lax.fori_loop(0, n_pad_ref[0], issue_pad, 0)
        lax.fori_loop(n_used_ref[0], n_tiles, lambda r, _: (tile_copy(r).start(), 0)[1], 0)
        lax.fori_loop(0, n_pad_ref[0], wait_pad, 0)
        lax.fori_loop(n_used_ref[0], n_tiles, lambda r, _: (tile_copy(r).wait(), 0)[1], 0)

    def issue(t, _):
        for k in range(TOP_K):
            _row_copy(h_ref, t, xs_ref, pos_ref[0, 0, k * tm + t], sem).start()
        return 0

    lax.fori_loop(0, tm, issue, 0, unroll=ROW_DMA_UNROLL)
    for _ in range(TOP_K):
        pltpu.make_async_copy(h_ref, xs_ref.at[pl.ds(0, tm)], sem).wait()


def _moe_scatter(pos, pad_rows, n_pad, n_used, h32, n_rows, *, tm, tr):
    t, w = h32.shape
    smem = pl.BlockSpec(memory_space=pltpu.SMEM)
    return pl.pallas_call(
        functools.partial(_moe_scatter_kernel, tm=tm),
        grid=(t // tm,),
        in_specs=[pl.BlockSpec((1, 1, TOP_K * tm), lambda i: (i, 0, 0), memory_space=pltpu.SMEM),
                  smem, smem, smem, pl.BlockSpec((tm, w), lambda i: (i, 0))],
        out_specs=pl.BlockSpec(memory_space=pl.ANY),
        out_shape=jax.ShapeDtypeStruct((n_rows, w), h32.dtype),
        scratch_shapes=[pltpu.VMEM((tr, w), h32.dtype), pltpu.SemaphoreType.DMA(()), pltpu.SemaphoreType.DMA(())],
        compiler_params=pltpu.CompilerParams(
            dimension_semantics=("arbitrary",), vmem_limit_bytes=VMEM_LIMIT, has_side_effects=True),
        name="moe_scatter",
    )(pos, pad_rows, n_pad, n_used, h32)


def _moe_experts_kernel(tile_expert_ref, n_used_ref, xs_ref, wgu_ref, wd_ref, ys_ref, *, d_expert):
    del tile_expert_ref
    used = pl.program_id(0) < n_used_ref[0]

    @pl.when(used)
    def _():
        ab = jnp.dot(xs_ref[...].astype(BF16), wgu_ref[0], preferred_element_type=F32)
        a = ab[:, :d_expert]
        act = (a * jax.nn.sigmoid(a) * ab[:, d_expert:]).astype(BF16)
        ys_ref[...] = jnp.dot(act, wd_ref[0], preferred_element_type=F32)

    @pl.when(jnp.logical_not(used))
    def _():
        ys_ref[...] = jnp.zeros_like(ys_ref)


def _moe_experts(tile_expert, n_used, xs, wgu, wd, *, tr):
    n_rows, w = xs.shape
    _, d, two_de = wgu.shape
    assert d == w
    last_used = lambda r, nu: jnp.minimum(r, nu[0] - 1)
    return pl.pallas_call(
        functools.partial(_moe_experts_kernel, d_expert=two_de // 2),
        grid_spec=pltpu.PrefetchScalarGridSpec(
            num_scalar_prefetch=2,
            grid=(n_rows // tr,),
            in_specs=[pl.BlockSpec((tr, w), lambda r, te, nu: (last_used(r, nu), 0)),
                      pl.BlockSpec((1, d, two_de), lambda r, te, nu: (te[r], 0, 0)),
                      pl.BlockSpec((1, two_de // 2, d), lambda r, te, nu: (te[r], 0, 0))],
            out_specs=pl.BlockSpec((tr, w), lambda r, te, nu: (r, 0))),
        out_shape=jax.ShapeDtypeStruct((n_rows, w), xs.dtype),
        compiler_params=pltpu.CompilerParams(
            dimension_semantics=("arbitrary",), vmem_limit_bytes=VMEM_LIMIT),
        name="moe_experts",
    )(tile_expert, n_used, xs, wgu, wd)


def _moe_combine_kernel(pos_ref, next_pos_ref, route_ref, h_ref, g_ref, b_ref, ys_ref, o_ref, y_buf, sems,
                        *, tm, alpha):
    i = pl.program_id(0)
    slot = i % 2

    def gather(p_ref, s):
        def issue(t, _):
            for k in range(TOP_K):
                _row_copy(ys_ref, p_ref[0, 0, k * tm + t], y_buf.at[s, k], t, sems.at[s]).start()
            return 0

        lax.fori_loop(0, tm, issue, 0, unroll=ROW_DMA_UNROLL)

    @pl.when(i == 0)
    def _():
        gather(pos_ref, slot)

    @pl.when(i + 1 < pl.num_programs(0))
    def _():
        gather(next_pos_ref, 1 - slot)

    for k in range(TOP_K):
        pltpu.make_async_copy(ys_ref.at[pl.ds(0, tm)], y_buf.at[slot, k], sems.at[slot]).wait()

    route = route_ref[...]
    lane = lax.broadcasted_iota(jnp.int32, route.shape, 1)
    f = jnp.zeros_like(h_ref)
    for k, w_lane in enumerate((ROUTE_W0, ROUTE_W1)):
        gate = jnp.sum(jnp.where(lane == w_lane, route, 0.0), axis=-1, keepdims=True)
        f = f + gate * y_buf[slot, k]
    o_ref[...] = _layer_norm(alpha * h_ref[...] + f, g_ref[...], b_ref[...])


def _moe_combine(pos, route, h32, ys, ln_g, ln_b, *, tm, alpha):
    t, d = h32.shape
    last = t // tm - 1
    pos_blk = lambda step: pl.BlockSpec((1, 1, TOP_K * tm), lambda i: (step(i), 0, 0), memory_space=pltpu.SMEM)
    return pl.pallas_call(
        functools.partial(_moe_combine_kernel, tm=tm, alpha=alpha),
        grid=(t // tm,),
        in_specs=[pos_blk(lambda i: i), pos_blk(lambda i: jnp.minimum(i + 1, last)),
                  pl.BlockSpec((tm, ROUTE_LANES), lambda i: (i, 0)),
                  pl.BlockSpec((tm, d), lambda i: (i, 0)),
                  _const_spec((1, d)), _const_spec((1, d)),
                  pl.BlockSpec(memory_space=pl.ANY)],
        out_specs=pl.BlockSpec((tm, d), lambda i: (i, 0)),
        out_shape=jax.ShapeDtypeStruct((t, d), F32),
        scratch_shapes=[pltpu.VMEM((2, TOP_K, tm) + ys.shape[1:], ys.dtype), pltpu.SemaphoreType.DMA((2,))],
        compiler_params=pltpu.CompilerParams(
            dimension_semantics=("arbitrary",), vmem_limit_bytes=VMEM_LIMIT),
        name="moe_combine",
    )(pos, pos, route, h32, ln_g, ln_b, ys)


def _moe(h32, route, counts, wgu, wd, ln_g, ln_b, *, tm, alpha):
    t, d = h32.shape
    n_experts = wgu.shape[0]
    tr = min(MOE_ROW_TILE, max(LANES, TOP_K * t // n_experts))
    assert t % tm == 0
    n_tiles = (TOP_K * t + n_experts * (tr - 1)) // tr

    def segment_of(i, seg_ends):
        return jnp.minimum(jnp.sum(i[:, None] >= seg_ends[None, :], axis=1), n_experts - 1).astype(jnp.int32)

    experts = jnp.arange(n_experts, dtype=jnp.int32)
    cnt = counts[0, :n_experts].astype(jnp.int32)
    padded = (cnt + tr - 1) // tr * tr
    ends = jnp.cumsum(padded)
    starts = ends - padded
    expert = route[:, ROUTE_E0:ROUTE_E1 + 1].astype(jnp.int32)
    rank = route[:, ROUTE_R0:ROUTE_R1 + 1].astype(jnp.int32)
    pos = jnp.sum(jnp.where(expert[..., None] == experts, starts, 0), axis=-1) + rank
    pos = pos.reshape(t // tm, tm, TOP_K).transpose(0, 2, 1).reshape(t // tm, 1, TOP_K * tm)
    tile_expert = segment_of(jnp.arange(n_tiles, dtype=jnp.int32) * tr, ends)
    n_used = ends[-1:] // tr
    pad_ends = jnp.cumsum(padded - cnt)
    i = jnp.arange(n_experts * (tr - 1), dtype=jnp.int32)
    seg = segment_of(i, pad_ends)
    pad_rows = jnp.minimum((starts + cnt)[seg] + i - (pad_ends - (padded - cnt))[seg], n_tiles * tr - 1)

    xs = _moe_scatter(pos, pad_rows, pad_ends[-1:], n_used, h32, n_tiles * tr, tm=tm, tr=tr)
    ys = _moe_experts(tile_expert, n_used, xs, wgu, wd, tr=tr)
    return _moe_combine(pos, route, h32, ys, ln_g, ln_b, tm=tm, alpha=alpha)


def _pick_tile(n, target):
    t = min(n, target)
    while n % t:
        t //= 2
    return t


def kernel(x_prompt, x_sample, cache_k, cache_v, state_conv, meta_tokens, w_in, conv_w, lambda_qk, subln_g, w_branch, w_out, ln1_g, ln1_b, w_group, b_group, w_router, b_router, w_gate_up, w_down, ln2_g, ln2_b):
    depth = w_in.shape[0]
    assert depth == 1, "single-layer step only"
    bp, seq, d = x_prompt.shape
    bs, s_len, _ = x_sample.shape
    n_meta = meta_tokens.shape[0]
    n_heads = cache_k.shape[3]
    dqk = cache_k.shape[4] // 2
    past = cache_k.shape[2] - n_meta
    n_groups = w_group.shape[-1]
    n_experts = w_router.shape[-1]
    assert n_experts + n_groups <= ROUTE_LANES
    q_scale = dqk ** -0.5
    assert math.frexp(q_scale)[0] == 0.5, "the score scale is folded into q; exact only for powers of two"
    alpha = (2.0 * depth) ** 0.25
    lam_init = 0.8 - 0.6 * math.exp(-0.3 * 0)

    w_in_b = w_in[0].astype(BF16)
    wb = w_branch[0].astype(BF16)
    w_out_b = w_out[0].astype(BF16)
    wgu_b = w_gate_up[0].astype(BF16)
    wd_b = w_down[0].astype(BF16)
    w_rt = jnp.zeros((d, ROUTE_LANES), F32).at[:, :n_experts].set(w_router[0]).at[
        :, n_experts:n_experts + n_groups].set(w_group[0])
    w_rt_hi = w_rt.astype(BF16)
    w_rt = jnp.concatenate([w_rt_hi, (w_rt - w_rt_hi.astype(F32)).astype(BF16)], axis=1)
    b_rt = jnp.zeros((1, ROUTE_LANES), F32).at[0, :n_experts].set(b_router[0]).at[
        0, n_experts:n_experts + n_groups].set(b_group[0])

    def hist_rows(rows):
        return jnp.pad(rows, ((0, 0), (HIST - rows.shape[1], 0), (0, 0)))

    proj = functools.partial(_proj_conv, w_in=w_in_b, conv_w=conv_w[0], wb0=wb[0], wqt=w_in_b[:, 3 * d:4 * d].T,
                             q_scale=q_scale)
    attn = functools.partial(_attention, lambda_qk[0], subln_g, n_heads=n_heads, lam_init=lam_init)
    tail = functools.partial(_tail, wb1=wb[1], w_out=w_out_b, ln_g=ln1_g, ln_b=ln1_b, w_rt=w_rt, b_rt=b_rt,
                             alpha=alpha, n_experts=n_experts, n_groups=n_groups)
    moe = functools.partial(_moe, wgu=wgu_b, wd=wd_b, ln_g=ln2_g, ln_b=ln2_b, alpha=alpha)

    mk32, mv32, _, mkb, mvb, _, _, mnc = proj(
        meta_tokens[None], jnp.zeros((1, HIST, d), F32), bb=1, ts=n_meta, transposed=False)
    ta = _pick_tile(seq, ATTN_TILE)
    assert n_meta <= LANES
    mk_pad = jnp.pad(mkb[0], ((0, -n_meta % BF16_SUBLANES), (0, 0)))
    mvt_pad = jnp.pad(mvb[0].T, ((0, 0), (0, LANES - n_meta)))
    k32, v32, qt, kb, vt, mc, sga, nc = proj(
        x_prompt, jnp.broadcast_to(mnc, (bp, HIST, d)), bb=1, ts=_pick_tile(seq, PROJ_TILE), transposed=True, ta=ta)
    ya = _attention_t(lambda_qk[0], subln_g, qt, mk_pad, mvt_pad, kb, vt,
                      n_heads=n_heads, n_meta=n_meta, lam_init=lam_init)
    t_p = bp * seq
    tm = _pick_tile(t_p, 512)
    routed = tail(x_prompt.reshape(t_p, d), mc.reshape(t_p, d), sga.reshape(t_p, d), ya.reshape(t_p, d), tm=tm)
    y_prompt = moe(*routed, tm=tm).reshape(bp, seq, d)

    sk32, sv32, sqb, skb, svb, smc, ssga, snc = proj(
        x_sample, hist_rows(state_conv[0]), bb=bs, ts=s_len, transposed=False)
    sya = attn(sqb, cache_k[0].reshape(bs, n_meta + past, d), cache_v[0].reshape(bs, n_meta + past, d),
               skb, svb, tq=s_len, frame0=past)
    t_s = bs * s_len
    srouted = tail(x_sample.reshape(t_s, d), smc.reshape(t_s, d), ssga.reshape(t_s, d), sya.reshape(t_s, d),
                   tm=t_s)
    y_sample = moe(*srouted, tm=t_s).reshape(bs, s_len, d)

    def with_meta(m, f):
        full = jnp.concatenate([jnp.broadcast_to(m, (bp, n_meta, d)), f], axis=1)
        return full.reshape(1, bp, n_meta + seq, n_heads, d // n_heads)

    return (y_prompt, y_sample,
            with_meta(mk32, k32), with_meta(mv32, v32), nc[None, :, HIST - 2:],
            sk32.reshape(1, bs, s_len, n_heads, d // n_heads),
            sv32.reshape(1, bs, s_len, n_heads, d // n_heads), snc[None, :, HIST - 2:])
```

```python
import functools
import math

import jax
import jax.numpy as jnp
from jax import lax
from jax.experimental import pallas as pl
from jax.experimental.pallas import tpu as pltpu

CHUNK = 64
CHUNK_SHIFT = CHUNK.bit_length() - 1
assert 1 << CHUNK_SHIFT == CHUNK
LANES = 128
BF16_SUBLANES = 16
LOG2_E = math.log2(math.e)
LN_EPS = 1e-5
RMS_EPS = 1e-5
HIST = 8
ROUTE_LANES = 128
ROUTE_E0, ROUTE_E1, ROUTE_W0, ROUTE_W1, ROUTE_R0, ROUTE_R1 = range(6)
TOP_K = 2
ROW_DMA_UNROLL = 8
MOE_ROW_TILE = 512
PROJ_TILE = 512
ATTN_TILE = 256
VMEM_LIMIT = 52 * 1024 * 1024

F32 = jnp.float32
BF16 = jnp.bfloat16


def _const_spec(shape):
    return pl.BlockSpec(shape, lambda *_: (0,) * len(shape), pipeline_mode=pl.Buffered(1))


def _layer_norm(x, g, b):
    mu = jnp.mean(x, axis=-1, keepdims=True)
    xc = x - mu
    var = jnp.mean(xc * xc, axis=-1, keepdims=True)
    return xc * lax.rsqrt(var + LN_EPS) * g + b


def _proj_conv_kernel(x_ref, c0_ref, win_ref, cw_ref, wb0_ref, wqt_ref,
                      k32_ref, v32_ref, q_ref, kb_ref, vb_ref, mc_ref, sga_ref, nc_ref,
                      carry_ref, *, bb, ts, d, q_scale, transposed):
    @pl.when(pl.program_id(1) == 0)
    def _():
        carry_ref[...] = c0_ref[...]

    xb = x_ref[...].reshape(bb * ts, d).astype(BF16)

    def proj(i):
        return jnp.dot(xb, win_ref[:, i * d:(i + 1) * d], preferred_element_type=F32)

    u = proj(2) * proj(0)
    cw = cw_ref[...]
    row = lax.broadcasted_iota(jnp.int32, (ts, d), 0)
    convs = []
    for b in range(bb):
        ub = u[b * ts:(b + 1) * ts]
        hist = carry_ref[b]
        h1 = hist[HIST - 1:HIST]
        h2 = hist[HIST - 2:HIST - 1]
        um1 = jnp.where(row == 0, h1, pltpu.roll(ub, 1, 0))
        um2 = jnp.where(row == 0, h2, jnp.where(row == 1, h1, pltpu.roll(ub, 2, 0)))
        convs.append(cw[0:1] * um2 + cw[1:2] * um1 + cw[2:3] * ub)
        carry_ref[b] = ub[ts - HIST:ts]
        nc_ref[b] = ub[ts - HIST:ts]
    conv = convs[0] if bb == 1 else jnp.concatenate(convs, axis=0)
    yc = (proj(1) * conv).astype(BF16)
    mc = jax.nn.sigmoid(proj(6)) * jnp.dot(yc, wb0_ref[...], preferred_element_type=F32)
    mc_ref[...] = mc.astype(BF16).reshape(bb, ts, d)
    sga_ref[...] = jax.nn.sigmoid(proj(7)).astype(BF16).reshape(bb, ts, d)
    k = proj(4)
    k32_ref[...] = k.reshape(bb, ts, d)
    kb_ref[...] = k.astype(BF16).reshape(bb, ts, d)
    v = proj(5)
    v32_ref[...] = v.reshape(bb, ts, d)
    if transposed:
        nt = (((1,), (1,)), ((), ()))
        qt = lax.dot_general(wqt_ref[...], xb, nt, preferred_element_type=F32)
        qt = (qt * (q_scale * LOG2_E)).astype(BF16)
        vt = v.T.astype(BF16)
        ta = q_ref.shape[3]
        for j in range(ts // ta):
            q_ref[0, j] = qt[:, j * ta:(j + 1) * ta]
            vb_ref[0, j] = vt[:, j * ta:(j + 1) * ta]
    else:
        q_ref[...] = (proj(3) * q_scale).astype(BF16).reshape(bb, ts, d)
        vb_ref[...] = v.astype(BF16).reshape(bb, ts, d)


def _proj_conv(x, c0, w_in, conv_w, wb0, wqt, *, bb, ts, q_scale, transposed, ta=None):
    nb, length, d = x.shape
    assert nb % bb == 0 and length % ts == 0 and ts % HIST == 0
    assert w_in.shape == (d, 8 * d), "all eight projection sections must be d_model wide"
    assert not transposed or (bb == 1 and ts % ta == 0)
    blk = pl.BlockSpec((bb, ts, d), lambda b, s: (b, s, 0))
    hist_blk = pl.BlockSpec((bb, HIST, d), lambda b, s: (b, 0, 0))
    f32_out = jax.ShapeDtypeStruct((nb, length, d), F32)
    bf_out = jax.ShapeDtypeStruct((nb, length, d), BF16)
    if transposed:
        qv_blk = pl.BlockSpec((1, ts // ta, d, ta), lambda b, s: (b, s, 0, 0))
        qv_out = jax.ShapeDtypeStruct((nb, length // ta, d, ta), BF16)
    else:
        qv_blk, qv_out = blk, bf_out
    return pl.pallas_call(
        functools.partial(_proj_conv_kernel, bb=bb, ts=ts, d=d, q_scale=q_scale, transposed=transposed),
        grid=(nb // bb, length // ts),
        in_specs=[blk, hist_blk, _const_spec((d, 8 * d)), _const_spec((3, d)), _const_spec((d, d)),
                  _const_spec((d, d))],
        out_specs=[blk, blk, qv_blk, blk, qv_blk, blk, blk, hist_blk],
        out_shape=[f32_out, f32_out, qv_out, bf_out, qv_out, bf_out, bf_out,
                   jax.ShapeDtypeStruct((nb, HIST, d), F32)],
        scratch_shapes=[pltpu.VMEM((bb, HIST, d), F32)],
        compiler_params=pltpu.CompilerParams(
            dimension_semantics=("parallel", "arbitrary"), vmem_limit_bytes=VMEM_LIMIT),
        name="proj_conv",
    )(x, c0, w_in, conv_w, wb0, wqt)


def _log2(n):
    assert n & (n - 1) == 0, "power of two expected"
    return n.bit_length() - 1


def _attn_cache_kernel(lqk_ref, g_ref, q_ref, ck_ref, cv_ref, kn_ref, vn_ref, o_ref,
                       *, n_heads, frame0, lam_init, dqk):
    lqk = lqk_ref[...]
    lam = (jnp.exp(jnp.sum(lqk[0:1] * lqk[1:2], axis=-1, keepdims=True))
           - jnp.exp(jnp.sum(lqk[2:3] * lqk[3:4], axis=-1, keepdims=True)) + lam_init)

    s_len, d = q_ref.shape[1:]
    dv = d // n_heads
    lane = lax.broadcasted_iota(jnp.int32, (s_len, dv), 1)
    zero = jnp.zeros((s_len, dv), BF16)

    def heads(x):
        return [x[:, h * dv:(h + 1) * dv] for h in range(n_heads)]

    qs = jnp.concatenate([piece for qh in heads(q_ref[0]) for piece in
                          (jnp.where(lane < dqk, qh, zero), jnp.where(lane >= dqk, qh, zero))], axis=0)
    k_new = jnp.concatenate(heads(kn_ref[0]), axis=0)
    v_new = jnp.concatenate(heads(vn_ref[0]), axis=0)
    row_head_shift = _log2(2 * s_len)

    def scores(k):
        return lax.dot_general(qs, k, (((1,), (1,)), ((), ())), preferred_element_type=F32)

    s_c = scores(ck_ref[0].astype(BF16))
    r = lax.broadcasted_iota(jnp.int32, s_c.shape, 0)
    c = lax.broadcasted_iota(jnp.int32, s_c.shape, 1)
    s_c = jnp.where((c & (n_heads - 1)) == lax.shift_right_logical(r, row_head_shift), s_c, -jnp.inf)

    s_n = scores(k_new)
    r = lax.broadcasted_iota(jnp.int32, s_n.shape, 0)
    c = lax.broadcasted_iota(jnp.int32, s_n.shape, 1)
    same_head = lax.shift_right_logical(c, _log2(s_len)) == lax.shift_right_logical(r, row_head_shift)
    q_chunk = lax.shift_right_logical(frame0 + (r & (s_len - 1)), CHUNK_SHIFT)
    k_chunk = lax.shift_right_logical(frame0 + (c & (s_len - 1)), CHUNK_SHIFT)
    s_n = jnp.where(same_head & (k_chunk <= q_chunk), s_n, -jnp.inf)

    m = jnp.maximum(jnp.max(s_c, axis=-1, keepdims=True), jnp.max(s_n, axis=-1, keepdims=True))
    p_c = jnp.exp(s_c - m)
    p_n = jnp.exp(s_n - m)
    l = jnp.sum(p_c, axis=-1, keepdims=True) + jnp.sum(p_n, axis=-1, keepdims=True)
    acc = (jnp.dot(p_c.astype(BF16), cv_ref[0].astype(BF16), preferred_element_type=F32)
           + jnp.dot(p_n.astype(BF16), v_new, preferred_element_type=F32))
    o = (acc / l).reshape(n_heads, 2, s_len, dv)
    o = o[:, 0] - lam * o[:, 1]
    o = o * lax.rsqrt(jnp.mean(o * o, axis=-1, keepdims=True) + RMS_EPS) * g_ref[...] * (1.0 - lam_init)
    o_ref[0] = jnp.concatenate([o[h] for h in range(n_heads)], axis=1).astype(o_ref.dtype)


def _attention_cache(lqk, subln_g, q, cache_k, cache_v, k_new, v_new, *, frame0, lam_init):
    nb, s_len, d = q.shape
    _, n_ctx, n_heads, dv = cache_k.shape
    dqk = lqk.shape[1]
    rows = n_ctx * n_heads
    assert cache_k.shape == cache_v.shape == (nb, n_ctx, n_heads, dv) and d == n_heads * dv == 2 * dqk * n_heads
    assert rows % LANES == 0 and (n_heads * s_len) % LANES == 0 and s_len % BF16_SUBLANES == 0
    _log2(n_heads)
    frames = pl.BlockSpec((1, s_len, d), lambda b: (b, 0, 0))
    cache = pl.BlockSpec((1, rows, dv), lambda b: (b, 0, 0))
    return pl.pallas_call(
        functools.partial(_attn_cache_kernel, n_heads=n_heads, frame0=frame0, lam_init=lam_init, dqk=dqk),
        grid=(nb,),
        in_specs=[pl.BlockSpec((4, dqk), lambda b: (0, 0)), pl.BlockSpec((1, dv), lambda b: (0, 0)),
                  frames, cache, cache, frames, frames],
        out_specs=frames,
        out_shape=jax.ShapeDtypeStruct((nb, s_len, d), BF16),
        compiler_params=pltpu.CompilerParams(dimension_semantics=("parallel",), vmem_limit_bytes=VMEM_LIMIT),
        name="diff_attention_cache",
    )(lqk, subln_g, q, cache_k.reshape(nb, rows, dv), cache_v.reshape(nb, rows, dv), k_new, v_new)


def _attn_t_kernel(lqk_ref, g_ref, mbias_ref, dbias_ref, qt_ref, mk_ref, mvt_ref, kf_ref, vt_ref, o_ref,
                   *, tq, n_tiles, lam_init, dqk):
    lqk = lqk_ref[...]
    lam = (jnp.exp(jnp.sum(lqk[0:1] * lqk[1:2], axis=-1, keepdims=True))
           - jnp.exp(jnp.sum(lqk[2:3] * lqk[3:4], axis=-1, keepdims=True)) + lam_init)

    dv = qt_ref.shape[2]
    row = lax.broadcasted_iota(jnp.int32, (dv, tq), 0)
    zero = jnp.zeros((dv, tq), BF16)
    ones = jnp.ones((BF16_SUBLANES, tq), BF16)
    meta_fill = jnp.zeros((LANES - mk_ref.shape[0], 2 * tq), BF16)

    def query_tile(q):
        n = q + 2
        qt = qt_ref[0, q]
        w = jnp.concatenate([jnp.where(row < dqk, qt, zero), jnp.where(row >= dqk, qt, zero)], axis=1)

        def scores(t):
            if t == 0:
                return jnp.dot(mk_ref[...], w, preferred_element_type=F32) + mbias_ref[...]
            s = jnp.dot(kf_ref[0, (t - 1) * tq:t * tq, :], w, preferred_element_type=F32)
            return s + dbias_ref[...] if t == n - 1 else s

        s = scores(0)
        for t in range(n):
            s_next = scores(t + 1) if t + 1 < n else None
            s_max = jnp.max(s, axis=0, keepdims=True)
            if t == 0:
                m = s_max
                p = jnp.concatenate([jnp.exp2(s - m).astype(BF16), meta_fill], axis=0)
                acc = jnp.dot(jnp.concatenate([mvt_ref[...], ones[:, :LANES]], axis=0), p,
                              preferred_element_type=F32)
            else:
                m_new = jnp.maximum(m, s_max)
                acc = jnp.exp2(m - m_new) * acc + jnp.dot(
                    jnp.concatenate([vt_ref[0, t - 1], ones], axis=0), jnp.exp2(s - m_new).astype(BF16),
                    preferred_element_type=F32)
                m = m_new
            s = s_next

        l = acc[dv:dv + 1]
        ot = acc[:dv, :tq] / l[:, :tq] - lam * (acc[:dv, tq:] / l[:, tq:])
        ot = ot * lax.rsqrt(jnp.mean(ot * ot, axis=0, keepdims=True) + RMS_EPS)
        o_ref[0, q * tq:(q + 1) * tq, :] = (ot.T * g_ref[...] * (1.0 - lam_init)).astype(o_ref.dtype)

    for q in range(n_tiles):
        query_tile(q)


def _attention_bias(tq, n_meta, meta_rows):
    r = jnp.arange(tq, dtype=jnp.int32)[:, None]
    c = jnp.arange(2 * tq, dtype=jnp.int32)[None, :] % tq
    meta = jnp.broadcast_to(r[:meta_rows] < n_meta, (meta_rows, 2 * tq))
    diag = (r // CHUNK) <= (c // CHUNK)
    neg = jnp.float32(-jnp.inf)
    return jnp.where(meta, 0.0, neg), jnp.where(diag, 0.0, neg)


def _attention_t(lqk, subln_g, qt, mk, mvt, kf, vt, *, n_heads, n_meta, lam_init):
    nb, n_tiles, d, tq = qt.shape
    length = n_tiles * tq
    dv = d // n_heads
    dqk = lqk.shape[1]
    meta_rows = mk.shape[0]
    assert dv == 2 * dqk and kf.shape == (nb, length, d) and vt.shape == qt.shape
    assert mk.shape[1] == d and mvt.shape == (d, LANES) and n_meta <= meta_rows <= LANES and tq % CHUNK == 0
    return pl.pallas_call(
        functools.partial(_attn_t_kernel, tq=tq, n_tiles=n_tiles, lam_init=lam_init, dqk=dqk),
        grid=(nb, n_heads),
        in_specs=[pl.BlockSpec((4, dqk), lambda b, h: (0, 0)),
                  pl.BlockSpec((1, dv), lambda b, h: (0, 0)),
                  _const_spec((meta_rows, 2 * tq)), _const_spec((tq, 2 * tq)),
                  pl.BlockSpec((1, n_tiles, dv, tq), lambda b, h: (b, 0, h, 0)),
                  pl.BlockSpec((meta_rows, dv), lambda b, h: (0, h)),
                  pl.BlockSpec((dv, LANES), lambda b, h: (h, 0)),
                  pl.BlockSpec((1, length, dv), lambda b, h: (b, 0, h)),
                  pl.BlockSpec((1, n_tiles, dv, tq), lambda b, h: (b, 0, h, 0))],
        out_specs=pl.BlockSpec((1, length, dv), lambda b, h: (b, 0, h)),
        out_shape=jax.ShapeDtypeStruct((nb, length, d), BF16),
        compiler_params=pltpu.CompilerParams(
            dimension_semantics=("parallel", "parallel"), vmem_limit_bytes=VMEM_LIMIT),
        name="diff_attention_t",
    )(lqk, subln_g, *_attention_bias(tq, n_meta, meta_rows), qt, mk, mvt, kf, vt)


def _tail_kernel(x_ref, mc_ref, sga_ref, ya_ref, wb1_ref, wout_ref, g_ref, b_ref, wrt_ref, brt_ref,
                 h32_ref, route_ref, counts_ref, *, alpha, n_experts, n_groups, n_sub):
    @pl.when(pl.program_id(0) == 0)
    def _():
        counts_ref[...] = jnp.zeros_like(counts_ref)

    tm = x_ref.shape[0]
    sub = tm // n_sub
    lane = lax.broadcasted_iota(jnp.int32, (sub, ROUTE_LANES), 1)
    big = jnp.int32(ROUTE_LANES)
    neg = -jnp.inf
    per_group = n_experts // n_groups
    r = lax.broadcasted_iota(jnp.int32, (sub, sub), 0)
    c = lax.broadcasted_iota(jnp.int32, (sub, sub), 1)
    earlier = jnp.where(c < r, 1.0, 0.0).astype(BF16)

    def first_max(v):
        vmax = jnp.max(v, axis=-1, keepdims=True)
        return vmax, jnp.min(jnp.where(v == vmax, lane, big), axis=-1, keepdims=True)

    counts = counts_ref[...]
    for g in range(n_sub):
        rows = slice(g * sub, (g + 1) * sub)
        merged = mc_ref[rows].astype(F32) + sga_ref[rows].astype(F32) * jnp.dot(
            ya_ref[rows], wb1_ref[...], preferred_element_type=F32)
        y = alpha * x_ref[rows] + jnp.dot(merged.astype(BF16), wout_ref[...], preferred_element_type=F32)
        h = _layer_norm(y, g_ref[...], b_ref[...])
        h32_ref[rows] = h

        h_hi = h.astype(BF16)
        h_lo = (h - h_hi.astype(F32)).astype(BF16)
        hw = jnp.dot(h_hi, wrt_ref[...], preferred_element_type=F32)
        logits = (hw[:, :ROUTE_LANES] + hw[:, ROUTE_LANES:]
                  + jnp.dot(h_lo, wrt_ref[:, :ROUTE_LANES], preferred_element_type=F32)) + brt_ref[...]

        is_group = (lane >= n_experts) & (lane < n_experts + n_groups)
        lg = jnp.where(is_group, logits, neg)
        eg = jnp.exp(lg - jnp.max(lg, axis=-1, keepdims=True))
        p_group = jnp.where(is_group, eg / jnp.sum(eg, axis=-1, keepdims=True), neg)
        p_g, g_lane = first_max(p_group)
        g_idx = g_lane - n_experts
        lo = g_idx * per_group
        le = jnp.where((lane >= lo) & (lane < lo + per_group), logits, neg)
        v1, i1 = first_max(le)
        v2, i2 = first_max(jnp.where(lane == i1, neg, le))
        e2 = jnp.exp(v2 - v1)
        w1 = p_g / (1.0 + e2)
        w2 = p_g * e2 / (1.0 + e2)

        onehot = jnp.where((lane == i1) | (lane == i2), 1.0, 0.0).astype(BF16)
        before = jnp.dot(earlier, onehot, preferred_element_type=F32) + counts
        r1 = jnp.sum(jnp.where(lane == i1, before, 0.0), axis=-1, keepdims=True)
        r2 = jnp.sum(jnp.where(lane == i2, before, 0.0), axis=-1, keepdims=True)
        counts = counts + jnp.sum(onehot.astype(F32), axis=0, keepdims=True)

        route = jnp.zeros_like(logits)
        for k, col in ((ROUTE_E0, i1.astype(F32)), (ROUTE_E1, i2.astype(F32)), (ROUTE_W0, w1), (ROUTE_W1, w2),
                       (ROUTE_R0, r1), (ROUTE_R1, r2)):
            route = jnp.where(lane == k, col, route)
        route_ref[rows] = route
    counts_ref[...] = counts


def _tail(x, mc, sga, ya, wb1, w_out, ln_g, ln_b, w_rt, b_rt, *, tm, alpha, n_experts, n_groups):
    t, d = x.shape
    assert t % tm == 0
    n_sub = 2 if tm % (2 * LANES) == 0 else 1
    row = lambda w: pl.BlockSpec((tm, w), lambda i: (i, 0))
    return pl.pallas_call(
        functools.partial(_tail_kernel, alpha=alpha, n_experts=n_experts, n_groups=n_groups, n_sub=n_sub),
        grid=(t // tm,),
        in_specs=[row(d), row(d), row(d), row(d), _const_spec((d, d)), _const_spec((d, d)),
                  _const_spec((1, d)), _const_spec((1, d)),
                  _const_spec((d, 2 * ROUTE_LANES)), _const_spec((1, ROUTE_LANES))],
        out_specs=[row(d), row(ROUTE_LANES), pl.BlockSpec((1, ROUTE_LANES), lambda i: (0, 0))],
        out_shape=[jax.ShapeDtypeStruct((t, d), F32), jax.ShapeDtypeStruct((t, ROUTE_LANES), F32),
                   jax.ShapeDtypeStruct((1, ROUTE_LANES), F32)],
        compiler_params=pltpu.CompilerParams(
            dimension_semantics=("arbitrary",), vmem_limit_bytes=VMEM_LIMIT),
        name="merge_ln_router",
    )(x, mc, sga, ya, wb1, w_out, ln_g, ln_b, w_rt, b_rt)


def _row_copy(src_ref, src_row, dst_ref, dst_row, sem):
    return pltpu.make_async_copy(src_ref.at[pl.ds(src_row, 1)], dst_ref.at[pl.ds(dst_row, 1)], sem)


def _moe_scatter_kernel(pos_ref, pad_rows_ref, n_pad_ref, n_used_ref, h_ref, xs_ref, zero_ref, stage_ref, sems,
                        pad_sem, *, tm):
    @pl.when(pl.program_id(0) == 0)
    def _():
        zero_ref[...] = jnp.zeros_like(zero_ref)
        tr = zero_ref.shape[0]
        n_tiles = xs_ref.shape[0] // tr

        def tile_copy(r):
            return pltpu.make_async_copy(zero_ref, xs_ref.at[pl.ds(pl.multiple_of(r * tr, tr), tr)], pad_sem)

        def issue_pad(j, _):
            _row_copy(zero_ref, 0, xs_ref, pad_rows_ref[j], pad_sem).start()
            return 0

        def wait_pad(j, _):
            _row_copy(zero_ref, 0, xs_ref, 0, pad_sem).wait()
            return 0

        lax.fori_loop(0, n_pad_ref[0], issue_pad, 0)
        lax.fori_loop(n_used_ref[0], n_tiles, lambda r, _: (tile_copy(r).start(), 0)[1], 0)
        lax.fori_loop(0, n_pad_ref[0], wait_pad, 0)
        lax.fori_loop(n_used_ref[0], n_tiles, lambda r, _: (tile_copy(r).wait(), 0)[1], 0)

    i = pl.program_id(0)
    slot = i % 2
    stage_ref[slot] = h_ref[...]

    def issue(t, _):
        for k in range(TOP_K):
            _row_copy(stage_ref.at[slot], t, xs_ref, pos_ref[0, 0, k * tm + t], sems.at[slot]).start()
        return 0

    def wait_slot(s):
        for _ in range(TOP_K):
            pltpu.make_async_copy(stage_ref.at[s], xs_ref.at[pl.ds(0, tm)], sems.at[s]).wait()

    lax.fori_loop(0, tm, issue, 0, unroll=ROW_DMA_UNROLL)
    pl.when(i > 0)(lambda: wait_slot(1 - slot))
    pl.when(i == pl.num_programs(0) - 1)(lambda: wait_slot(slot))


def _moe_scatter(pos, pad_rows, n_pad, n_used, h32, n_rows, *, tm, tr):
    t, w = h32.shape
    smem = pl.BlockSpec(memory_space=pltpu.SMEM)
    return pl.pallas_call(
        functools.partial(_moe_scatter_kernel, tm=tm),
        grid=(t // tm,),
        in_specs=[pl.BlockSpec((1, 1, TOP_K * tm), lambda i: (i, 0, 0), memory_space=pltpu.SMEM),
                  smem, smem, smem, pl.BlockSpec((tm, w), lambda i: (i, 0))],
        out_specs=pl.BlockSpec(memory_space=pl.ANY),
        out_shape=jax.ShapeDtypeStruct((n_rows, w), h32.dtype),
        scratch_shapes=[pltpu.VMEM((tr, w), h32.dtype), pltpu.VMEM((2, tm, w), h32.dtype),
                        pltpu.SemaphoreType.DMA((2,)), pltpu.SemaphoreType.DMA(())],
        compiler_params=pltpu.CompilerParams(
            dimension_semantics=("arbitrary",), vmem_limit_bytes=VMEM_LIMIT, has_side_effects=True),
        name="moe_scatter",
    )(pos, pad_rows, n_pad, n_used, h32)


def _moe_experts_kernel(tile_expert_ref, n_used_ref, xs_ref, wgu_ref, wd_ref, ys_ref, *, d_expert):
    del tile_expert_ref
    used = pl.program_id(0) < n_used_ref[0]

    @pl.when(used)
    def _():
        ab = jnp.dot(xs_ref[...].astype(BF16), wgu_ref[0], preferred_element_type=F32)
        a = ab[:, :d_expert]
        act = (a * jax.nn.sigmoid(a) * ab[:, d_expert:]).astype(BF16)
        ys_ref[...] = jnp.dot(act, wd_ref[0], preferred_element_type=F32)

    @pl.when(jnp.logical_not(used))
    def _():
        ys_ref[...] = jnp.zeros_like(ys_ref)


def _moe_experts(tile_expert, n_used, xs, wgu, wd, *, tr):
    n_rows, w = xs.shape
    _, d, two_de = wgu.shape
    assert d == w
    last_used = lambda r, nu: jnp.minimum(r, nu[0] - 1)
    return pl.pallas_call(
        functools.partial(_moe_experts_kernel, d_expert=two_de // 2),
        grid_spec=pltpu.PrefetchScalarGridSpec(
            num_scalar_prefetch=2,
            grid=(n_rows // tr,),
            in_specs=[pl.BlockSpec((tr, w), lambda r, te, nu: (last_used(r, nu), 0)),
                      pl.BlockSpec((1, d, two_de), lambda r, te, nu: (te[r], 0, 0)),
                      pl.BlockSpec((1, two_de // 2, d), lambda r, te, nu: (te[r], 0, 0))],
            out_specs=pl.BlockSpec((tr, w), lambda r, te, nu: (r, 0))),
        out_shape=jax.ShapeDtypeStruct((n_rows, w), xs.dtype),
        compiler_params=pltpu.CompilerParams(
            dimension_semantics=("arbitrary",), vmem_limit_bytes=VMEM_LIMIT),
        name="moe_experts",
    )(tile_expert, n_used, xs, wgu, wd)


def _moe_combine_kernel(pos_ref, next_pos_ref, route_ref, h_ref, g_ref, b_ref, ys_ref, o_ref, y_buf, sems,
                        *, tm, alpha):
    i = pl.program_id(0)
    slot = i % 2

    def gather(p_ref, s):
        def issue(t, _):
            for k in range(TOP_K):
                _row_copy(ys_ref, p_ref[0, 0, k * tm + t], y_buf.at[s, k], t, sems.at[s]).start()
            return 0

        lax.fori_loop(0, tm, issue, 0, unroll=ROW_DMA_UNROLL)

    @pl.when(i == 0)
    def _():
        gather(pos_ref, slot)

    @pl.when(i + 1 < pl.num_programs(0))
    def _():
        gather(next_pos_ref, 1 - slot)

    for k in range(TOP_K):
        pltpu.make_async_copy(ys_ref.at[pl.ds(0, tm)], y_buf.at[slot, k], sems.at[slot]).wait()

    route = route_ref[...]
    lane = lax.broadcasted_iota(jnp.int32, route.shape, 1)
    f = jnp.zeros_like(h_ref)
    for k, w_lane in enumerate((ROUTE_W0, ROUTE_W1)):
        gate = jnp.sum(jnp.where(lane == w_lane, route, 0.0), axis=-1, keepdims=True)
        f = f + gate * y_buf[slot, k]
    o_ref[...] = _layer_norm(alpha * h_ref[...] + f, g_ref[...], b_ref[...])


def _moe_combine(pos, route, h32, ys, ln_g, ln_b, *, tm, alpha):
    t, d = h32.shape
    last = t // tm - 1
    pos_blk = lambda step: pl.BlockSpec((1, 1, TOP_K * tm), lambda i: (step(i), 0, 0), memory_space=pltpu.SMEM)
    return pl.pallas_call(
        functools.partial(_moe_combine_kernel, tm=tm, alpha=alpha),
        grid=(t // tm,),
        in_specs=[pos_blk(lambda i: i), pos_blk(lambda i: jnp.minimum(i + 1, last)),
                  pl.BlockSpec((tm, ROUTE_LANES), lambda i: (i, 0)),
                  pl.BlockSpec((tm, d), lambda i: (i, 0)),
                  _const_spec((1, d)), _const_spec((1, d)),
                  pl.BlockSpec(memory_space=pl.ANY)],
        out_specs=pl.BlockSpec((tm, d), lambda i: (i, 0)),
        out_shape=jax.ShapeDtypeStruct((t, d), F32),
        scratch_shapes=[pltpu.VMEM((2, TOP_K, tm) + ys.shape[1:], ys.dtype), pltpu.SemaphoreType.DMA((2,))],
        compiler_params=pltpu.CompilerParams(
            dimension_semantics=("arbitrary",), vmem_limit_bytes=VMEM_LIMIT),
        name="moe_combine",
    )(pos, pos, route, h32, ln_g, ln_b, ys)


def _moe(h32, route, counts, wgu, wd, ln_g, ln_b, *, tm, alpha):
    t, d = h32.shape
    n_experts = wgu.shape[0]
    tr = min(MOE_ROW_TILE, max(LANES // 2, TOP_K * t // n_experts))
    assert t % tm == 0
    n_tiles = (TOP_K * t + n_experts * (tr - 1)) // tr

    def segment_of(i, seg_ends):
        return jnp.minimum(jnp.sum(i[:, None] >= seg_ends[None, :], axis=1), n_experts - 1).astype(jnp.int32)

    experts = jnp.arange(n_experts, dtype=jnp.int32)
    cnt = counts[0, :n_experts].astype(jnp.int32)
    padded = (cnt + tr - 1) // tr * tr
    ends = jnp.cumsum(padded)
    starts = ends - padded
    expert = route[:, ROUTE_E0:ROUTE_E1 + 1].astype(jnp.int32)
    rank = route[:, ROUTE_R0:ROUTE_R1 + 1].astype(jnp.int32)
    pos = jnp.sum(jnp.where(expert[..., None] == experts, starts, 0), axis=-1) + rank
    pos = pos.reshape(t // tm, tm, TOP_K).transpose(0, 2, 1).reshape(t // tm, 1, TOP_K * tm)
    tile_expert = segment_of(jnp.arange(n_tiles, dtype=jnp.int32) * tr, ends)
    n_used = ends[-1:] // tr
    pad_ends = jnp.cumsum(padded - cnt)
    i = jnp.arange(n_experts * (tr - 1), dtype=jnp.int32)
    seg = segment_of(i, pad_ends)
    pad_rows = jnp.minimum((starts + cnt)[seg] + i - (pad_ends - (padded - cnt))[seg], n_tiles * tr - 1)

    xs = _moe_scatter(pos, pad_rows, pad_ends[-1:], n_used, h32, n_tiles * tr, tm=tm, tr=tr)
    ys = _moe_experts(tile_expert, n_used, xs, wgu, wd, tr=tr)
    return _moe_combine(pos, route, h32, ys, ln_g, ln_b, tm=tm, alpha=alpha)


def _pick_tile(n, target):
    t = min(n, target)
    while n % t:
        t //= 2
    return t


def kernel(x_prompt, x_sample, cache_k, cache_v, state_conv, meta_tokens, w_in, conv_w, lambda_qk, subln_g, w_branch, w_out, ln1_g, ln1_b, w_group, b_group, w_router, b_router, w_gate_up, w_down, ln2_g, ln2_b):
    depth = w_in.shape[0]
    assert depth == 1, "single-layer step only"
    bp, seq, d = x_prompt.shape
    bs, s_len, _ = x_sample.shape
    n_meta = meta_tokens.shape[0]
    n_heads = cache_k.shape[3]
    dqk = cache_k.shape[4] // 2
    past = cache_k.shape[2] - n_meta
    n_groups = w_group.shape[-1]
    n_experts = w_router.shape[-1]
    assert n_experts + n_groups <= ROUTE_LANES
    q_scale = dqk ** -0.5
    assert math.frexp(q_scale)[0] == 0.5, "the score scale is folded into q; exact only for powers of two"
    alpha = (2.0 * depth) ** 0.25
    lam_init = 0.8 - 0.6 * math.exp(-0.3 * 0)

    w_in_b = w_in[0].astype(BF16)
    wb = w_branch[0].astype(BF16)
    w_out_b = w_out[0].astype(BF16)
    wgu_b = w_gate_up[0].astype(BF16)
    wd_b = w_down[0].astype(BF16)
    w_rt = jnp.zeros((d, ROUTE_LANES), F32).at[:, :n_experts].set(w_router[0]).at[
        :, n_experts:n_experts + n_groups].set(w_group[0])
    w_rt_hi = w_rt.astype(BF16)
    w_rt = jnp.concatenate([w_rt_hi, (w_rt - w_rt_hi.astype(F32)).astype(BF16)], axis=1)
    b_rt = jnp.zeros((1, ROUTE_LANES), F32).at[0, :n_experts].set(b_router[0]).at[
        0, n_experts:n_experts + n_groups].set(b_group[0])

    def hist_rows(rows):
        return jnp.pad(rows, ((0, 0), (HIST - rows.shape[1], 0), (0, 0)))

    proj = functools.partial(_proj_conv, w_in=w_in_b, conv_w=conv_w[0], wb0=wb[0], wqt=w_in_b[:, 3 * d:4 * d].T,
                             q_scale=q_scale)
    tail = functools.partial(_tail, wb1=wb[1], w_out=w_out_b, ln_g=ln1_g, ln_b=ln1_b, w_rt=w_rt, b_rt=b_rt,
                             alpha=alpha, n_experts=n_experts, n_groups=n_groups)
    moe = functools.partial(_moe, wgu=wgu_b, wd=wd_b, ln_g=ln2_g, ln_b=ln2_b, alpha=alpha)

    mk32, mv32, _, mkb, mvb, _, _, mnc = proj(
        meta_tokens[None], jnp.zeros((1, HIST, d), F32), bb=1, ts=n_meta, transposed=False)
    ta = _pick_tile(seq, ATTN_TILE)
    assert n_meta <= LANES
    mk_pad = jnp.pad(mkb[0], ((0, -n_meta % BF16_SUBLANES), (0, 0)))
    mvt_pad = jnp.pad(mvb[0].T, ((0, 0), (0, LANES - n_meta)))
    k32, v32, qt, kb, vt, mc, sga, nc = proj(
        x_prompt, jnp.broadcast_to(mnc, (bp, HIST, d)), bb=1, ts=_pick_tile(seq, PROJ_TILE), transposed=True, ta=ta)
    ya = _attention_t(lambda_qk[0], subln_g, qt, mk_pad, mvt_pad, kb, vt,
                      n_heads=n_heads, n_meta=n_meta, lam_init=lam_init)
    t_p = bp * seq
    tm = _pick_tile(t_p, 512)
    routed = tail(x_prompt.reshape(t_p, d), mc.reshape(t_p, d), sga.reshape(t_p, d), ya.reshape(t_p, d), tm=tm)
    y_prompt = moe(*routed, tm=tm).reshape(bp, seq, d)

    sk32, sv32, sqb, skb, svb, smc, ssga, snc = proj(
        x_sample, hist_rows(state_conv[0]), bb=bs, ts=s_len, transposed=False)
    sya = _attention_cache(lambda_qk[0], subln_g, sqb, cache_k[0], cache_v[0], skb, svb,
                           frame0=past, lam_init=lam_init)
    t_s = bs * s_len
    srouted = tail(x_sample.reshape(t_s, d), smc.reshape(t_s, d), ssga.reshape(t_s, d), sya.reshape(t_s, d),
                   tm=t_s)
    y_sample = moe(*srouted, tm=t_s).reshape(bs, s_len, d)

    def with_meta(m, f):
        full = jnp.concatenate([jnp.broadcast_to(m, (bp, n_meta, d)), f], axis=1)
        return full.reshape(1, bp, n_meta + seq, n_heads, d // n_heads)

    return (y_prompt, y_sample,
            with_meta(mk32, k32), with_meta(mv32, v32), nc[None, :, HIST - 2:],
            sk32.reshape(1, bs, s_len, n_heads, d // n_heads),
            sv32.reshape(1, bs, s_len, n_heads, d // n_heads), snc[None, :, HIST - 2:])
```

```python
import functools
import math

import jax
import jax.numpy as jnp
from jax import lax
from jax.experimental import pallas as pl
from jax.experimental.pallas import tpu as pltpu

CHUNK = 64
CHUNK_SHIFT = CHUNK.bit_length() - 1
assert 1 << CHUNK_SHIFT == CHUNK
LANES = 128
BF16_SUBLANES = 16
LOG2_E = math.log2(math.e)
LN_EPS = 1e-5
RMS_EPS = 1e-5
HIST = 8
ROUTE_LANES = 128
ROUTE_E0, ROUTE_E1, ROUTE_W0, ROUTE_W1, ROUTE_R0, ROUTE_R1 = range(6)
TOP_K = 2
ROW_DMA_UNROLL = 8
MOE_ROW_TILE = 512
MOE_TOKEN_TILE = 1024
PROJ_TILE = 512
ATTN_TILE = 512
VMEM_LIMIT = 52 * 1024 * 1024

F32 = jnp.float32
BF16 = jnp.bfloat16


def _const_spec(shape):
    return pl.BlockSpec(shape, lambda *_: (0,) * len(shape), pipeline_mode=pl.Buffered(1))


def _layer_norm(x, g, b):
    mu = jnp.mean(x, axis=-1, keepdims=True)
    xc = x - mu
    var = jnp.mean(xc * xc, axis=-1, keepdims=True)
    return xc * lax.rsqrt(var + LN_EPS) * g + b


def _proj_conv_kernel(x_ref, c0_ref, win_ref, cw_ref, wb0_ref, wqt_ref,
                      k32_ref, v32_ref, q_ref, kb_ref, vb_ref, mc_ref, sga_ref, nc_ref,
                      carry_ref, *, bb, ts, d, q_scale, transposed):
    @pl.when(pl.program_id(1) == 0)
    def _():
        carry_ref[...] = c0_ref[...]

    n = q_ref.shape[3] if transposed else ts
    cw = cw_ref[...]
    row = lax.broadcasted_iota(jnp.int32, (n, d), 0)
    hists = [carry_ref[b] for b in range(bb)]
    for j in range(ts // n):
        rows = slice(j * n, (j + 1) * n)
        xb = (x_ref[0, rows, :] if transposed else x_ref[...].reshape(bb * ts, d)).astype(BF16)

        def proj(i):
            return jnp.dot(xb, win_ref[:, i * d:(i + 1) * d], preferred_element_type=F32)

        def put(ref, val):
            if transposed:
                ref[0, rows, :] = val
            else:
                ref[...] = val.reshape(bb, ts, d)

        u = proj(2) * proj(0)
        convs = []
        for b in range(bb):
            ub = u[b * n:(b + 1) * n]
            h1 = hists[b][HIST - 1:HIST]
            h2 = hists[b][HIST - 2:HIST - 1]
            um1 = jnp.where(row == 0, h1, pltpu.roll(ub, 1, 0))
            um2 = jnp.where(row == 0, h2, jnp.where(row == 1, h1, pltpu.roll(ub, 2, 0)))
            convs.append(cw[0:1] * um2 + cw[1:2] * um1 + cw[2:3] * ub)
            hists[b] = ub[n - HIST:n]
        conv = convs[0] if bb == 1 else jnp.concatenate(convs, axis=0)
        yc = (proj(1) * conv).astype(BF16)
        mc = jax.nn.sigmoid(proj(6)) * jnp.dot(yc, wb0_ref[...], preferred_element_type=F32)
        put(mc_ref, mc.astype(BF16))
        put(sga_ref, jax.nn.sigmoid(proj(7)).astype(BF16))
        k = proj(4)
        put(k32_ref, k)
        put(kb_ref, k.astype(BF16))
        v = proj(5)
        put(v32_ref, v)
        if transposed:
            qt = lax.dot_general(wqt_ref[...], xb, (((1,), (1,)), ((), ())), preferred_element_type=F32)
            q_ref[0, j] = (qt * (q_scale * LOG2_E)).astype(BF16)
            vb_ref[0, j] = v.T.astype(BF16)
        else:
            put(q_ref, (proj(3) * q_scale).astype(BF16))
            put(vb_ref, v.astype(BF16))
    for b in range(bb):
        carry_ref[b] = hists[b]
        nc_ref[b] = hists[b]


def _proj_conv(x, c0, w_in, conv_w, wb0, wqt, *, bb, ts, q_scale, transposed, ta=None):
    nb, length, d = x.shape
    assert nb % bb == 0 and length % ts == 0 and ts % HIST == 0
    assert w_in.shape == (d, 8 * d), "all eight projection sections must be d_model wide"
    assert not transposed or (bb == 1 and ts % ta == 0)
    blk = pl.BlockSpec((bb, ts, d), lambda b, s: (b, s, 0))
    hist_blk = pl.BlockSpec((bb, HIST, d), lambda b, s: (b, 0, 0))
    f32_out = jax.ShapeDtypeStruct((nb, length, d), F32)
    bf_out = jax.ShapeDtypeStruct((nb, length, d), BF16)
    if transposed:
        qv_blk = pl.BlockSpec((1, ts // ta, d, ta), lambda b, s: (b, s, 0, 0))
        qv_out = jax.ShapeDtypeStruct((nb, length // ta, d, ta), BF16)
    else:
        qv_blk, qv_out = blk, bf_out
    return pl.pallas_call(
        functools.partial(_proj_conv_kernel, bb=bb, ts=ts, d=d, q_scale=q_scale, transposed=transposed),
        grid=(nb // bb, length // ts),
        in_specs=[blk, hist_blk, _const_spec((d, 8 * d)), _const_spec((3, d)), _const_spec((d, d)),
                  _const_spec((d, d))],
        out_specs=[blk, blk, qv_blk, blk, qv_blk, blk, blk, hist_blk],
        out_shape=[f32_out, f32_out, qv_out, bf_out, qv_out, bf_out, bf_out,
                   jax.ShapeDtypeStruct((nb, HIST, d), F32)],
        scratch_shapes=[pltpu.VMEM((bb, HIST, d), F32)],
        compiler_params=pltpu.CompilerParams(
            dimension_semantics=("parallel", "arbitrary"), vmem_limit_bytes=VMEM_LIMIT),
        name="proj_conv",
    )(x, c0, w_in, conv_w, wb0, wqt)


def _log2(n):
    assert n & (n - 1) == 0, "power of two expected"
    return n.bit_length() - 1


def _attn_cache_kernel(lqk_ref, g_ref, q_ref, ck_ref, cv_ref, kn_ref, vn_ref, o_ref,
                       *, n_heads, frame0, lam_init, dqk):
    lqk = lqk_ref[...]
    lam = (jnp.exp(jnp.sum(lqk[0:1] * lqk[1:2], axis=-1, keepdims=True))
           - jnp.exp(jnp.sum(lqk[2:3] * lqk[3:4], axis=-1, keepdims=True)) + lam_init)

    s_len, d = q_ref.shape[1:]
    dv = d // n_heads
    lane = lax.broadcasted_iota(jnp.int32, (s_len, dv), 1)
    zero = jnp.zeros((s_len, dv), BF16)

    def heads(x):
        return [x[:, h * dv:(h + 1) * dv] for h in range(n_heads)]

    qs = jnp.concatenate([piece for qh in heads(q_ref[0]) for piece in
                          (jnp.where(lane < dqk, qh, zero), jnp.where(lane >= dqk, qh, zero))], axis=0)
    k_new = jnp.concatenate(heads(kn_ref[0]), axis=0)
    v_new = jnp.concatenate(heads(vn_ref[0]), axis=0)
    row_head_shift = _log2(2 * s_len)

    def scores(k):
        return lax.dot_general(qs, k, (((1,), (1,)), ((), ())), preferred_element_type=F32)

    s_c = scores(ck_ref[0].astype(BF16))
    r = lax.broadcasted_iota(jnp.int32, s_c.shape, 0)
    c = lax.broadcasted_iota(jnp.int32, s_c.shape, 1)
    s_c = jnp.where((c & (n_heads - 1)) == lax.shift_right_logical(r, row_head_shift), s_c, -jnp.inf)

    s_n = scores(k_new)
    r = lax.broadcasted_iota(jnp.int32, s_n.shape, 0)
    c = lax.broadcasted_iota(jnp.int32, s_n.shape, 1)
    same_head = lax.shift_right_logical(c, _log2(s_len)) == lax.shift_right_logical(r, row_head_shift)
    q_chunk = lax.shift_right_logical(frame0 + (r & (s_len - 1)), CHUNK_SHIFT)
    k_chunk = lax.shift_right_logical(frame0 + (c & (s_len - 1)), CHUNK_SHIFT)
    s_n = jnp.where(same_head & (k_chunk <= q_chunk), s_n, -jnp.inf)

    m = jnp.maximum(jnp.max(s_c, axis=-1, keepdims=True), jnp.max(s_n, axis=-1, keepdims=True))
    p_c = jnp.exp(s_c - m)
    p_n = jnp.exp(s_n - m)
    l = jnp.sum(p_c, axis=-1, keepdims=True) + jnp.sum(p_n, axis=-1, keepdims=True)
    acc = (jnp.dot(p_c.astype(BF16), cv_ref[0].astype(BF16), preferred_element_type=F32)
           + jnp.dot(p_n.astype(BF16), v_new, preferred_element_type=F32))
    o = (acc / l).reshape(n_heads, 2, s_len, dv)
    o = o[:, 0] - lam * o[:, 1]
    o = o * lax.rsqrt(jnp.mean(o * o, axis=-1, keepdims=True) + RMS_EPS) * g_ref[...] * (1.0 - lam_init)
    o_ref[0] = jnp.concatenate([o[h] for h in range(n_heads)], axis=1).astype(o_ref.dtype)


def _attention_cache(lqk, subln_g, q, cache_k, cache_v, k_new, v_new, *, frame0, lam_init):
    nb, s_len, d = q.shape
    _, n_ctx, n_heads, dv = cache_k.shape
    dqk = lqk.shape[1]
    rows = n_ctx * n_heads
    assert cache_k.shape == cache_v.shape == (nb, n_ctx, n_heads, dv) and d == n_heads * dv == 2 * dqk * n_heads
    assert rows % LANES == 0 and (n_heads * s_len) % LANES == 0 and s_len % BF16_SUBLANES == 0
    _log2(n_heads)
    frames = pl.BlockSpec((1, s_len, d), lambda b: (b, 0, 0))
    cache = pl.BlockSpec((1, rows, dv), lambda b: (b, 0, 0))
    return pl.pallas_call(
        functools.partial(_attn_cache_kernel, n_heads=n_heads, frame0=frame0, lam_init=lam_init, dqk=dqk),
        grid=(nb,),
        in_specs=[pl.BlockSpec((4, dqk), lambda b: (0, 0)), pl.BlockSpec((1, dv), lambda b: (0, 0)),
                  frames, cache, cache, frames, frames],
        out_specs=frames,
        out_shape=jax.ShapeDtypeStruct((nb, s_len, d), BF16),
        compiler_params=pltpu.CompilerParams(dimension_semantics=("parallel",), vmem_limit_bytes=VMEM_LIMIT),
        name="diff_attention_cache",
    )(lqk, subln_g, q, cache_k.reshape(nb, rows, dv), cache_v.reshape(nb, rows, dv), k_new, v_new)


def _attn_t_kernel(lqk_ref, g_ref, mbias_ref, dbias_ref, qt_ref, mk_ref, mvt_ref, kf_ref, vt_ref, o_ref,
                   *, tq, n_tiles, lam_init, dqk):
    lqk = lqk_ref[...]
    lam = (jnp.exp(jnp.sum(lqk[0:1] * lqk[1:2], axis=-1, keepdims=True))
           - jnp.exp(jnp.sum(lqk[2:3] * lqk[3:4], axis=-1, keepdims=True)) + lam_init)

    dv = qt_ref.shape[2]
    row = lax.broadcasted_iota(jnp.int32, (dv, tq), 0)
    zero = jnp.zeros((dv, tq), BF16)
    ones = jnp.ones((BF16_SUBLANES, tq), BF16)
    meta_fill = jnp.zeros((LANES - mk_ref.shape[0], 2 * tq), BF16)

    def query_tile(q):
        n = q + 2
        qt = qt_ref[0, q]
        w = jnp.concatenate([jnp.where(row < dqk, qt, zero), jnp.where(row >= dqk, qt, zero)], axis=1)

        def scores(t):
            if t == 0:
                return jnp.dot(mk_ref[...], w, preferred_element_type=F32) + mbias_ref[...]
            s = jnp.dot(kf_ref[0, (t - 1) * tq:t * tq, :], w, preferred_element_type=F32)
            return s + dbias_ref[...] if t == n - 1 else s

        s = scores(0)
        for t in range(n):
            s_next = scores(t + 1) if t + 1 < n else None
            s_max = jnp.max(s, axis=0, keepdims=True)
            if t == 0:
                m = s_max
                p = jnp.concatenate([jnp.exp2(s - m).astype(BF16), meta_fill], axis=0)
                acc = jnp.dot(jnp.concatenate([mvt_ref[...], ones[:, :LANES]], axis=0), p,
                              preferred_element_type=F32)
            else:
                m_new = jnp.maximum(m, s_max)
                acc = jnp.exp2(m - m_new) * acc + jnp.dot(
                    jnp.concatenate([vt_ref[0, t - 1], ones], axis=0), jnp.exp2(s - m_new).astype(BF16),
                    preferred_element_type=F32)
                m = m_new
            s = s_next

        l = acc[dv:dv + 1]
        ot = acc[:dv, :tq] / l[:, :tq] - lam * (acc[:dv, tq:] / l[:, tq:])
        ot = ot * lax.rsqrt(jnp.mean(ot * ot, axis=0, keepdims=True) + RMS_EPS)
        o_ref[0, q * tq:(q + 1) * tq, :] = (ot.T * g_ref[...] * (1.0 - lam_init)).astype(o_ref.dtype)

    for q in range(n_tiles):
        query_tile(q)


def _attention_bias(tq, n_meta, meta_rows):
    r = jnp.arange(tq, dtype=jnp.int32)[:, None]
    c = jnp.arange(2 * tq, dtype=jnp.int32)[None, :] % tq
    meta = jnp.broadcast_to(r[:meta_rows] < n_meta, (meta_rows, 2 * tq))
    diag = (r // CHUNK) <= (c // CHUNK)
    neg = jnp.float32(-jnp.inf)
    return jnp.where(meta, 0.0, neg), jnp.where(diag, 0.0, neg)


def _attention_t(lqk, subln_g, qt, mk, mvt, kf, vt, *, n_heads, n_meta, lam_init):
    nb, n_tiles, d, tq = qt.shape
    length = n_tiles * tq
    dv = d // n_heads
    dqk = lqk.shape[1]
    meta_rows = mk.shape[0]
    assert dv == 2 * dqk and kf.shape == (nb, length, d) and vt.shape == qt.shape
    assert mk.shape[1] == d and mvt.shape == (d, LANES) and n_meta <= meta_rows <= LANES and tq % CHUNK == 0
    return pl.pallas_call(
        functools.partial(_attn_t_kernel, tq=tq, n_tiles=n_tiles, lam_init=lam_init, dqk=dqk),
        grid=(nb, n_heads),
        in_specs=[pl.BlockSpec((4, dqk), lambda b, h: (0, 0)),
                  pl.BlockSpec((1, dv), lambda b, h: (0, 0)),
                  _const_spec((meta_rows, 2 * tq)), _const_spec((tq, 2 * tq)),
                  pl.BlockSpec((1, n_tiles, dv, tq), lambda b, h: (b, 0, h, 0)),
                  pl.BlockSpec((meta_rows, dv), lambda b, h: (0, h)),
                  pl.BlockSpec((dv, LANES), lambda b, h: (h, 0)),
                  pl.BlockSpec((1, length, dv), lambda b, h: (b, 0, h)),
                  pl.BlockSpec((1, n_tiles, dv, tq), lambda b, h: (b, 0, h, 0))],
        out_specs=pl.BlockSpec((1, length, dv), lambda b, h: (b, 0, h)),
        out_shape=jax.ShapeDtypeStruct((nb, length, d), BF16),
        compiler_params=pltpu.CompilerParams(
            dimension_semantics=("parallel", "parallel"), vmem_limit_bytes=VMEM_LIMIT),
        name="diff_attention_t",
    )(lqk, subln_g, *_attention_bias(tq, n_meta, meta_rows), qt, mk, mvt, kf, vt)


def _tail_kernel(x_ref, mc_ref, sga_ref, ya_ref, wb1_ref, wout_ref, g_ref, b_ref, wrt_ref, brt_ref,
                 h32_ref, route_ref, counts_ref, *, alpha, n_experts, n_groups, n_sub):
    @pl.when(pl.program_id(0) == 0)
    def _():
        counts_ref[...] = jnp.zeros_like(counts_ref)

    tm = x_ref.shape[0]
    sub = tm // n_sub
    lane = lax.broadcasted_iota(jnp.int32, (sub, ROUTE_LANES), 1)
    big = jnp.int32(ROUTE_LANES)
    neg = -jnp.inf
    per_group = n_experts // n_groups
    r = lax.broadcasted_iota(jnp.int32, (sub, sub), 0)
    c = lax.broadcasted_iota(jnp.int32, (sub, sub), 1)
    earlier = jnp.where(c < r, 1.0, 0.0).astype(BF16)

    def first_max(v):
        vmax = jnp.max(v, axis=-1, keepdims=True)
        return vmax, jnp.min(jnp.where(v == vmax, lane, big), axis=-1, keepdims=True)

    counts = counts_ref[...]
    for g in range(n_sub):
        rows = slice(g * sub, (g + 1) * sub)
        merged = mc_ref[rows].astype(F32) + sga_ref[rows].astype(F32) * jnp.dot(
            ya_ref[rows], wb1_ref[...], preferred_element_type=F32)
        y = alpha * x_ref[rows] + jnp.dot(merged.astype(BF16), wout_ref[...], preferred_element_type=F32)
        h = _layer_norm(y, g_ref[...], b_ref[...])
        h32_ref[rows] = h

        h_hi = h.astype(BF16)
        h_lo = (h - h_hi.astype(F32)).astype(BF16)
        hw = jnp.dot(h_hi, wrt_ref[...], preferred_element_type=F32)
        logits = (hw[:, :ROUTE_LANES] + hw[:, ROUTE_LANES:]
                  + jnp.dot(h_lo, wrt_ref[:, :ROUTE_LANES], preferred_element_type=F32)) + brt_ref[...]

        is_group = (lane >= n_experts) & (lane < n_experts + n_groups)
        lg = jnp.where(is_group, logits, neg)
        eg = jnp.exp(lg - jnp.max(lg, axis=-1, keepdims=True))
        p_group = jnp.where(is_group, eg / jnp.sum(eg, axis=-1, keepdims=True), neg)
        p_g, g_lane = first_max(p_group)
        g_idx = g_lane - n_experts
        lo = g_idx * per_group
        le = jnp.where((lane >= lo) & (lane < lo + per_group), logits, neg)
        v1, i1 = first_max(le)
        v2, i2 = first_max(jnp.where(lane == i1, neg, le))
        e2 = jnp.exp(v2 - v1)
        w1 = p_g / (1.0 + e2)
        w2 = p_g * e2 / (1.0 + e2)

        onehot = jnp.where((lane == i1) | (lane == i2), 1.0, 0.0).astype(BF16)
        before = jnp.dot(earlier, onehot, preferred_element_type=F32) + counts
        r1 = jnp.sum(jnp.where(lane == i1, before, 0.0), axis=-1, keepdims=True)
        r2 = jnp.sum(jnp.where(lane == i2, before, 0.0), axis=-1, keepdims=True)
        counts = counts + jnp.sum(onehot.astype(F32), axis=0, keepdims=True)

        route = jnp.zeros_like(logits)
        for k, col in ((ROUTE_E0, i1.astype(F32)), (ROUTE_E1, i2.astype(F32)), (ROUTE_W0, w1), (ROUTE_W1, w2),
                       (ROUTE_R0, r1), (ROUTE_R1, r2)):
            route = jnp.where(lane == k, col, route)
        route_ref[rows] = route
    counts_ref[...] = counts


def _tail(x, mc, sga, ya, wb1, w_out, ln_g, ln_b, w_rt, b_rt, *, tm, alpha, n_experts, n_groups):
    t, d = x.shape
    assert t % tm == 0
    n_sub = 2 if tm % (2 * LANES) == 0 else 1
    row = lambda w: pl.BlockSpec((tm, w), lambda i: (i, 0))
    return pl.pallas_call(
        functools.partial(_tail_kernel, alpha=alpha, n_experts=n_experts, n_groups=n_groups, n_sub=n_sub),
        grid=(t // tm,),
        in_specs=[row(d), row(d), row(d), row(d), _const_spec((d, d)), _const_spec((d, d)),
                  _const_spec((1, d)), _const_spec((1, d)),
                  _const_spec((d, 2 * ROUTE_LANES)), _const_spec((1, ROUTE_LANES))],
        out_specs=[row(d), row(ROUTE_LANES), pl.BlockSpec((1, ROUTE_LANES), lambda i: (0, 0))],
        out_shape=[jax.ShapeDtypeStruct((t, d), F32), jax.ShapeDtypeStruct((t, ROUTE_LANES), F32),
                   jax.ShapeDtypeStruct((1, ROUTE_LANES), F32)],
        compiler_params=pltpu.CompilerParams(
            dimension_semantics=("arbitrary",), vmem_limit_bytes=VMEM_LIMIT),
        name="merge_ln_router",
    )(x, mc, sga, ya, wb1, w_out, ln_g, ln_b, w_rt, b_rt)


def _row_copy(src_ref, src_row, dst_ref, dst_row, sem):
    return pltpu.make_async_copy(src_ref.at[pl.ds(src_row, 1)], dst_ref.at[pl.ds(dst_row, 1)], sem)


def _moe_scatter_kernel(pos_ref, pad_rows_ref, n_pad_ref, n_used_ref, h_ref, xs_ref, zero_ref, stage_ref, sems,
                        pad_sem, *, tm):
    @pl.when(pl.program_id(0) == 0)
    def _():
        zero_ref[...] = jnp.zeros_like(zero_ref)
        tr = zero_ref.shape[0]
        n_tiles = xs_ref.shape[0] // tr

        def tile_copy(r):
            return pltpu.make_async_copy(zero_ref, xs_ref.at[pl.ds(pl.multiple_of(r * tr, tr), tr)], pad_sem)

        def issue_pad(j, _):
            _row_copy(zero_ref, 0, xs_ref, pad_rows_ref[j], pad_sem).start()
            return 0

        def wait_pad(j, _):
            _row_copy(zero_ref, 0, xs_ref, 0, pad_sem).wait()
            return 0

        lax.fori_loop(0, n_pad_ref[0], issue_pad, 0)
        lax.fori_loop(n_used_ref[0], n_tiles, lambda r, _: (tile_copy(r).start(), 0)[1], 0)
        lax.fori_loop(0, n_pad_ref[0], wait_pad, 0)
        lax.fori_loop(n_used_ref[0], n_tiles, lambda r, _: (tile_copy(r).wait(), 0)[1], 0)

    i = pl.program_id(0)
    slot = i % 2
    stage_ref[slot] = h_ref[...]

    def issue(t, _):
        for k in range(TOP_K):
            _row_copy(stage_ref.at[slot], t, xs_ref, pos_ref[0, 0, k * tm + t], sems.at[slot]).start()
        return 0

    def wait_slot(s):
        for _ in range(TOP_K):
            pltpu.make_async_copy(stage_ref.at[s], xs_ref.at[pl.ds(0, tm)], sems.at[s]).wait()

    lax.fori_loop(0, tm, issue, 0, unroll=ROW_DMA_UNROLL)
    pl.when(i > 0)(lambda: wait_slot(1 - slot))
    pl.when(i == pl.num_programs(0) - 1)(lambda: wait_slot(slot))


def _moe_scatter(pos, pad_rows, n_pad, n_used, h32, n_rows, *, tm, tr):
    t, w = h32.shape
    smem = pl.BlockSpec(memory_space=pltpu.SMEM)
    return pl.pallas_call(
        functools.partial(_moe_scatter_kernel, tm=tm),
        grid=(t // tm,),
        in_specs=[pl.BlockSpec((1, 1, TOP_K * tm), lambda i: (i, 0, 0), memory_space=pltpu.SMEM),
                  smem, smem, smem, pl.BlockSpec((tm, w), lambda i: (i, 0))],
        out_specs=pl.BlockSpec(memory_space=pl.ANY),
        out_shape=jax.ShapeDtypeStruct((n_rows, w), h32.dtype),
        scratch_shapes=[pltpu.VMEM((tr, w), h32.dtype), pltpu.VMEM((2, tm, w), h32.dtype),
                        pltpu.SemaphoreType.DMA((2,)), pltpu.SemaphoreType.DMA(())],
        compiler_params=pltpu.CompilerParams(
            dimension_semantics=("arbitrary",), vmem_limit_bytes=VMEM_LIMIT, has_side_effects=True),
        name="moe_scatter",
    )(pos, pad_rows, n_pad, n_used, h32)


def _moe_experts_kernel(tile_expert_ref, n_used_ref, xs_ref, wgu_ref, wd_ref, ys_ref, *, d_expert):
    del tile_expert_ref
    used = pl.program_id(0) < n_used_ref[0]

    @pl.when(used)
    def _():
        ab = jnp.dot(xs_ref[...].astype(BF16), wgu_ref[0], preferred_element_type=F32)
        a = ab[:, :d_expert]
        act = (a * jax.nn.sigmoid(a) * ab[:, d_expert:]).astype(BF16)
        ys_ref[...] = jnp.dot(act, wd_ref[0], preferred_element_type=F32)

    @pl.when(jnp.logical_not(used))
    def _():
        ys_ref[...] = jnp.zeros_like(ys_ref)


def _moe_experts(tile_expert, n_used, xs, wgu, wd, *, tr):
    n_rows, w = xs.shape
    _, d, two_de = wgu.shape
    assert d == w
    last_used = lambda r, nu: jnp.minimum(r, nu[0] - 1)
    return pl.pallas_call(
        functools.partial(_moe_experts_kernel, d_expert=two_de // 2),
        grid_spec=pltpu.PrefetchScalarGridSpec(
            num_scalar_prefetch=2,
            grid=(n_rows // tr,),
            in_specs=[pl.BlockSpec((tr, w), lambda r, te, nu: (last_used(r, nu), 0)),
                      pl.BlockSpec((1, d, two_de), lambda r, te, nu: (te[r], 0, 0)),
                      pl.BlockSpec((1, two_de // 2, d), lambda r, te, nu: (te[r], 0, 0))],
            out_specs=pl.BlockSpec((tr, w), lambda r, te, nu: (r, 0))),
        out_shape=jax.ShapeDtypeStruct((n_rows, w), xs.dtype),
        compiler_params=pltpu.CompilerParams(
            dimension_semantics=("arbitrary",), vmem_limit_bytes=VMEM_LIMIT),
        name="moe_experts",
    )(tile_expert, n_used, xs, wgu, wd)


def _moe_combine_kernel(pos_ref, next_pos_ref, route_ref, h_ref, g_ref, b_ref, ys_ref, o_ref, y_buf, sems,
                        *, tm, alpha):
    i = pl.program_id(0)
    slot = i % 2

    def gather(p_ref, s):
        def issue(t, _):
            for k in range(TOP_K):
                _row_copy(ys_ref, p_ref[0, 0, k * tm + t], y_buf.at[s, k], t, sems.at[s]).start()
            return 0

        lax.fori_loop(0, tm, issue, 0, unroll=ROW_DMA_UNROLL)

    @pl.when(i == 0)
    def _():
        gather(pos_ref, slot)

    @pl.when(i + 1 < pl.num_programs(0))
    def _():
        gather(next_pos_ref, 1 - slot)

    for k in range(TOP_K):
        pltpu.make_async_copy(ys_ref.at[pl.ds(0, tm)], y_buf.at[slot, k], sems.at[slot]).wait()

    route = route_ref[...]
    lane = lax.broadcasted_iota(jnp.int32, route.shape, 1)
    f = jnp.zeros_like(h_ref)
    for k, w_lane in enumerate((ROUTE_W0, ROUTE_W1)):
        gate = jnp.sum(jnp.where(lane == w_lane, route, 0.0), axis=-1, keepdims=True)
        f = f + gate * y_buf[slot, k]
    o_ref[...] = _layer_norm(alpha * h_ref[...] + f, g_ref[...], b_ref[...])


def _moe_combine(pos, route, h32, ys, ln_g, ln_b, *, tm, alpha):
    t, d = h32.shape
    last = t // tm - 1
    pos_blk = lambda step: pl.BlockSpec((1, 1, TOP_K * tm), lambda i: (step(i), 0, 0), memory_space=pltpu.SMEM)
    return pl.pallas_call(
        functools.partial(_moe_combine_kernel, tm=tm, alpha=alpha),
        grid=(t // tm,),
        in_specs=[pos_blk(lambda i: i), pos_blk(lambda i: jnp.minimum(i + 1, last)),
                  pl.BlockSpec((tm, ROUTE_LANES), lambda i: (i, 0)),
                  pl.BlockSpec((tm, d), lambda i: (i, 0)),
                  _const_spec((1, d)), _const_spec((1, d)),
                  pl.BlockSpec(memory_space=pl.ANY)],
        out_specs=pl.BlockSpec((tm, d), lambda i: (i, 0)),
        out_shape=jax.ShapeDtypeStruct((t, d), F32),
        scratch_shapes=[pltpu.VMEM((2, TOP_K, tm) + ys.shape[1:], ys.dtype), pltpu.SemaphoreType.DMA((2,))],
        compiler_params=pltpu.CompilerParams(
            dimension_semantics=("arbitrary",), vmem_limit_bytes=VMEM_LIMIT),
        name="moe_combine",
    )(pos, pos, route, h32, ln_g, ln_b, ys)


def _moe(h32, route, counts, wgu, wd, ln_g, ln_b, *, tm, alpha):
    t, d = h32.shape
    n_experts = wgu.shape[0]
    tr = min(MOE_ROW_TILE, max(LANES // 2, TOP_K * t // n_experts))
    assert t % tm == 0
    n_tiles = (TOP_K * t + n_experts * (tr - 1)) // tr

    def segment_of(i, seg_ends):
        return jnp.minimum(jnp.sum(i[:, None] >= seg_ends[None, :], axis=1), n_experts - 1).astype(jnp.int32)

    experts = jnp.arange(n_experts, dtype=jnp.int32)
    cnt = counts[0, :n_experts].astype(jnp.int32)
    padded = (cnt + tr - 1) // tr * tr
    ends = jnp.cumsum(padded)
    starts = ends - padded
    expert = route[:, ROUTE_E0:ROUTE_E1 + 1].astype(jnp.int32)
    rank = route[:, ROUTE_R0:ROUTE_R1 + 1].astype(jnp.int32)
    pos = jnp.sum(jnp.where(expert[..., None] == experts, starts, 0), axis=-1) + rank
    pos = pos.reshape(t // tm, tm, TOP_K).transpose(0, 2, 1).reshape(t // tm, 1, TOP_K * tm)
    tile_expert = segment_of(jnp.arange(n_tiles, dtype=jnp.int32) * tr, ends)
    n_used = ends[-1:] // tr
    pad_ends = jnp.cumsum(padded - cnt)
    i = jnp.arange(n_experts * (tr - 1), dtype=jnp.int32)
    seg = segment_of(i, pad_ends)
    pad_rows = jnp.minimum((starts + cnt)[seg] + i - (pad_ends - (padded - cnt))[seg], n_tiles * tr - 1)

    xs = _moe_scatter(pos, pad_rows, pad_ends[-1:], n_used, h32, n_tiles * tr, tm=tm, tr=tr)
    ys = _moe_experts(tile_expert, n_used, xs, wgu, wd, tr=tr)
    return _moe_combine(pos, route, h32, ys, ln_g, ln_b, tm=tm, alpha=alpha)


def _pick_tile(n, target):
    t = min(n, target)
    while n % t:
        t //= 2
    return t


def kernel(x_prompt, x_sample, cache_k, cache_v, state_conv, meta_tokens, w_in, conv_w, lambda_qk, subln_g, w_branch, w_out, ln1_g, ln1_b, w_group, b_group, w_router, b_router, w_gate_up, w_down, ln2_g, ln2_b):
    depth = w_in.shape[0]
    assert depth == 1, "single-layer step only"
    bp, seq, d = x_prompt.shape
    bs, s_len, _ = x_sample.shape
    n_meta = meta_tokens.shape[0]
    n_heads = cache_k.shape[3]
    dqk = cache_k.shape[4] // 2
    past = cache_k.shape[2] - n_meta
    n_groups = w_group.shape[-1]
    n_experts = w_router.shape[-1]
    assert n_experts + n_groups <= ROUTE_LANES
    q_scale = dqk ** -0.5
    assert math.frexp(q_scale)[0] == 0.5, "the score scale is folded into q; exact only for powers of two"
    alpha = (2.0 * depth) ** 0.25
    lam_init = 0.8 - 0.6 * math.exp(-0.3 * 0)

    w_in_b = w_in[0].astype(BF16)
    wb = w_branch[0].astype(BF16)
    w_out_b = w_out[0].astype(BF16)
    wgu_b = w_gate_up[0].astype(BF16)
    wd_b = w_down[0].astype(BF16)
    w_rt = jnp.zeros((d, ROUTE_LANES), F32).at[:, :n_experts].set(w_router[0]).at[
        :, n_experts:n_experts + n_groups].set(w_group[0])
    w_rt_hi = w_rt.astype(BF16)
    w_rt = jnp.concatenate([w_rt_hi, (w_rt - w_rt_hi.astype(F32)).astype(BF16)], axis=1)
    b_rt = jnp.zeros((1, ROUTE_LANES), F32).at[0, :n_experts].set(b_router[0]).at[
        0, n_experts:n_experts + n_groups].set(b_group[0])

    def hist_rows(rows):
        return jnp.pad(rows, ((0, 0), (HIST - rows.shape[1], 0), (0, 0)))

    proj = functools.partial(_proj_conv, w_in=w_in_b, conv_w=conv_w[0], wb0=wb[0], wqt=w_in_b[:, 3 * d:4 * d].T,
                             q_scale=q_scale)
    tail = functools.partial(_tail, wb1=wb[1], w_out=w_out_b, ln_g=ln1_g, ln_b=ln1_b, w_rt=w_rt, b_rt=b_rt,
                             alpha=alpha, n_experts=n_experts, n_groups=n_groups)
    moe = functools.partial(_moe, wgu=wgu_b, wd=wd_b, ln_g=ln2_g, ln_b=ln2_b, alpha=alpha)

    mk32, mv32, _, mkb, mvb, _, _, mnc = proj(
        meta_tokens[None], jnp.zeros((1, HIST, d), F32), bb=1, ts=n_meta, transposed=False)
    ta = _pick_tile(seq, ATTN_TILE)
    assert n_meta <= LANES
    mk_pad = jnp.pad(mkb[0], ((0, -n_meta % BF16_SUBLANES), (0, 0)))
    mvt_pad = jnp.pad(mvb[0].T, ((0, 0), (0, LANES - n_meta)))
    k32, v32, qt, kb, vt, mc, sga, nc = proj(
        x_prompt, jnp.broadcast_to(mnc, (bp, HIST, d)), bb=1, ts=_pick_tile(seq, PROJ_TILE), transposed=True, ta=ta)
    ya = _attention_t(lambda_qk[0], subln_g, qt, mk_pad, mvt_pad, kb, vt,
                      n_heads=n_heads, n_meta=n_meta, lam_init=lam_init)
    t_p = bp * seq
    tm = _pick_tile(t_p, 512)
    routed = tail(x_prompt.reshape(t_p, d), mc.reshape(t_p, d), sga.reshape(t_p, d), ya.reshape(t_p, d), tm=tm)
    y_prompt = moe(*routed, tm=_pick_tile(t_p, MOE_TOKEN_TILE)).reshape(bp, seq, d)

    sk32, sv32, sqb, skb, svb, smc, ssga, snc = proj(
        x_sample, hist_rows(state_conv[0]), bb=bs, ts=s_len, transposed=False)
    sya = _attention_cache(lambda_qk[0], subln_g, sqb, cache_k[0], cache_v[0], skb, svb,
                           frame0=past, lam_init=lam_init)
    t_s = bs * s_len
    srouted = tail(x_sample.reshape(t_s, d), smc.reshape(t_s, d), ssga.reshape(t_s, d), sya.reshape(t_s, d),
                   tm=t_s)
    y_sample = moe(*srouted, tm=t_s).reshape(bs, s_len, d)

    def with_meta(m, f):
        full = jnp.concatenate([jnp.broadcast_to(m, (bp, n_meta, d)), f], axis=1)
        return full.reshape(1, bp, n_meta + seq, n_heads, d // n_heads)

    return (y_prompt, y_sample,
            with_meta(mk32, k32), with_meta(mv32, v32), nc[None, :, HIST - 2:],
            sk32.reshape(1, bs, s_len, n_heads, d // n_heads),
            sv32.reshape(1, bs, s_len, n_heads, d // n_heads), snc[None, :, HIST - 2:])
```

```python
import functools
import math

import jax
import jax.numpy as jnp
from jax import lax
from jax.experimental import pallas as pl
from jax.experimental.pallas import tpu as pltpu

CHUNK = 64
CHUNK_SHIFT = CHUNK.bit_length() - 1
assert 1 << CHUNK_SHIFT == CHUNK
LANES = 128
BF16_SUBLANES = 16
LOG2_E = math.log2(math.e)
LN_EPS = 1e-5
RMS_EPS = 1e-5
HIST = 8
ROUTE_LANES = 128
ROUTE_E0, ROUTE_E1, ROUTE_W0, ROUTE_W1, ROUTE_R0, ROUTE_R1 = range(6)
TOP_K = 2
ROW_DMA_UNROLL = 8
MOE_ROW_TILE = 512
MOE_TOKEN_TILE = 1024
TAIL_TILE = 1024
TAIL_GROUP = 256
PROJ_TILE = 512
ATTN_TILE = 512
VMEM_LIMIT = 52 * 1024 * 1024

F32 = jnp.float32
BF16 = jnp.bfloat16


def _const_spec(shape):
    return pl.BlockSpec(shape, lambda *_: (0,) * len(shape), pipeline_mode=pl.Buffered(1))


def _layer_norm(x, g, b):
    mu = jnp.mean(x, axis=-1, keepdims=True)
    xc = x - mu
    var = jnp.mean(xc * xc, axis=-1, keepdims=True)
    return xc * lax.rsqrt(var + LN_EPS) * g + b


def _proj_conv_kernel(x_ref, c0_ref, win_ref, cw_ref, wb0_ref, wqt_ref,
                      k32_ref, v32_ref, q_ref, kb_ref, vb_ref, mc_ref, sga_ref, nc_ref,
                      carry_ref, *, bb, ts, d, q_scale, transposed):
    @pl.when(pl.program_id(1) == 0)
    def _():
        carry_ref[...] = c0_ref[...]

    n = q_ref.shape[3] if transposed else ts
    cw = cw_ref[...]
    row = lax.broadcasted_iota(jnp.int32, (n, d), 0)
    hists = [carry_ref[b] for b in range(bb)]
    for j in range(ts // n):
        rows = slice(j * n, (j + 1) * n)
        xb = (x_ref[0, rows, :] if transposed else x_ref[...].reshape(bb * ts, d)).astype(BF16)

        def proj(i):
            return jnp.dot(xb, win_ref[:, i * d:(i + 1) * d], preferred_element_type=F32)

        def put(ref, val):
            if transposed:
                ref[0, rows, :] = val
            else:
                ref[...] = val.reshape(bb, ts, d)

        u = proj(2) * proj(0)
        convs = []
        for b in range(bb):
            ub = u[b * n:(b + 1) * n]
            h1 = hists[b][HIST - 1:HIST]
            h2 = hists[b][HIST - 2:HIST - 1]
            um1 = jnp.where(row == 0, h1, pltpu.roll(ub, 1, 0))
            um2 = jnp.where(row == 0, h2, jnp.where(row == 1, h1, pltpu.roll(ub, 2, 0)))
            convs.append(cw[0:1] * um2 + cw[1:2] * um1 + cw[2:3] * ub)
            hists[b] = ub[n - HIST:n]
        conv = convs[0] if bb == 1 else jnp.concatenate(convs, axis=0)
        yc = (proj(1) * conv).astype(BF16)
        mc = jax.nn.sigmoid(proj(6)) * jnp.dot(yc, wb0_ref[...], preferred_element_type=F32)
        put(mc_ref, mc.astype(BF16))
        put(sga_ref, jax.nn.sigmoid(proj(7)).astype(BF16))
        k = proj(4)
        put(k32_ref, k)
        put(kb_ref, k.astype(BF16))
        v = proj(5)
        put(v32_ref, v)
        if transposed:
            qt = lax.dot_general(wqt_ref[...], xb, (((1,), (1,)), ((), ())), preferred_element_type=F32)
            q_ref[0, j] = (qt * (q_scale * LOG2_E)).astype(BF16)
            vb_ref[0, j] = v.T.astype(BF16)
        else:
            put(q_ref, (proj(3) * q_scale).astype(BF16))
            put(vb_ref, v.astype(BF16))
    for b in range(bb):
        carry_ref[b] = hists[b]
        nc_ref[b] = hists[b]


def _proj_conv(x, c0, w_in, conv_w, wb0, wqt, *, bb, ts, q_scale, transposed, ta=None):
    nb, length, d = x.shape
    assert nb % bb == 0 and length % ts == 0 and ts % HIST == 0
    assert w_in.shape == (d, 8 * d), "all eight projection sections must be d_model wide"
    assert not transposed or (bb == 1 and ts % ta == 0)
    blk = pl.BlockSpec((bb, ts, d), lambda b, s: (b, s, 0))
    hist_blk = pl.BlockSpec((bb, HIST, d), lambda b, s: (b, 0, 0))
    f32_out = jax.ShapeDtypeStruct((nb, length, d), F32)
    bf_out = jax.ShapeDtypeStruct((nb, length, d), BF16)
    if transposed:
        qv_blk = pl.BlockSpec((1, ts // ta, d, ta), lambda b, s: (b, s, 0, 0))
        qv_out = jax.ShapeDtypeStruct((nb, length // ta, d, ta), BF16)
    else:
        qv_blk, qv_out = blk, bf_out
    return pl.pallas_call(
        functools.partial(_proj_conv_kernel, bb=bb, ts=ts, d=d, q_scale=q_scale, transposed=transposed),
        grid=(nb // bb, length // ts),
        in_specs=[blk, hist_blk, _const_spec((d, 8 * d)), _const_spec((3, d)), _const_spec((d, d)),
                  _const_spec((d, d))],
        out_specs=[blk, blk, qv_blk, blk, qv_blk, blk, blk, hist_blk],
        out_shape=[f32_out, f32_out, qv_out, bf_out, qv_out, bf_out, bf_out,
                   jax.ShapeDtypeStruct((nb, HIST, d), F32)],
        scratch_shapes=[pltpu.VMEM((bb, HIST, d), F32)],
        compiler_params=pltpu.CompilerParams(
            dimension_semantics=("parallel", "arbitrary"), vmem_limit_bytes=VMEM_LIMIT),
        name="proj_conv",
    )(x, c0, w_in, conv_w, wb0, wqt)


def _log2(n):
    assert n & (n - 1) == 0, "power of two expected"
    return n.bit_length() - 1


def _attn_cache_kernel(lqk_ref, g_ref, q_ref, ck_ref, cv_ref, kn_ref, vn_ref, o_ref,
                       *, n_heads, frame0, lam_init, dqk):
    lqk = lqk_ref[...]
    lam = (jnp.exp(jnp.sum(lqk[0:1] * lqk[1:2], axis=-1, keepdims=True))
           - jnp.exp(jnp.sum(lqk[2:3] * lqk[3:4], axis=-1, keepdims=True)) + lam_init)

    s_len, d = q_ref.shape[1:]
    dv = d // n_heads
    lane = lax.broadcasted_iota(jnp.int32, (s_len, dv), 1)
    zero = jnp.zeros((s_len, dv), BF16)

    def heads(x):
        return [x[:, h * dv:(h + 1) * dv] for h in range(n_heads)]

    qs = jnp.concatenate([piece for qh in heads(q_ref[0]) for piece in
                          (jnp.where(lane < dqk, qh, zero), jnp.where(lane >= dqk, qh, zero))], axis=0)
    k_new = jnp.concatenate(heads(kn_ref[0]), axis=0)
    v_new = jnp.concatenate(heads(vn_ref[0]), axis=0)
    row_head_shift = _log2(2 * s_len)

    def scores(k):
        return lax.dot_general(qs, k, (((1,), (1,)), ((), ())), preferred_element_type=F32)

    s_c = scores(ck_ref[0].astype(BF16))
    r = lax.broadcasted_iota(jnp.int32, s_c.shape, 0)
    c = lax.broadcasted_iota(jnp.int32, s_c.shape, 1)
    s_c = jnp.where((c & (n_heads - 1)) == lax.shift_right_logical(r, row_head_shift), s_c, -jnp.inf)

    s_n = scores(k_new)
    r = lax.broadcasted_iota(jnp.int32, s_n.shape, 0)
    c = lax.broadcasted_iota(jnp.int32, s_n.shape, 1)
    same_head = lax.shift_right_logical(c, _log2(s_len)) == lax.shift_right_logical(r, row_head_shift)
    q_chunk = lax.shift_right_logical(frame0 + (r & (s_len - 1)), CHUNK_SHIFT)
    k_chunk = lax.shift_right_logical(frame0 + (c & (s_len - 1)), CHUNK_SHIFT)
    s_n = jnp.where(same_head & (k_chunk <= q_chunk), s_n, -jnp.inf)

    m = jnp.maximum(jnp.max(s_c, axis=-1, keepdims=True), jnp.max(s_n, axis=-1, keepdims=True))
    p_c = jnp.exp(s_c - m)
    p_n = jnp.exp(s_n - m)
    l = jnp.sum(p_c, axis=-1, keepdims=True) + jnp.sum(p_n, axis=-1, keepdims=True)
    acc = (jnp.dot(p_c.astype(BF16), cv_ref[0].astype(BF16), preferred_element_type=F32)
           + jnp.dot(p_n.astype(BF16), v_new, preferred_element_type=F32))
    o = (acc / l).reshape(n_heads, 2, s_len, dv)
    o = o[:, 0] - lam * o[:, 1]
    o = o * lax.rsqrt(jnp.mean(o * o, axis=-1, keepdims=True) + RMS_EPS) * g_ref[...] * (1.0 - lam_init)
    o_ref[0] = jnp.concatenate([o[h] for h in range(n_heads)], axis=1).astype(o_ref.dtype)


def _attention_cache(lqk, subln_g, q, cache_k, cache_v, k_new, v_new, *, frame0, lam_init):
    nb, s_len, d = q.shape
    _, n_ctx, n_heads, dv = cache_k.shape
    dqk = lqk.shape[1]
    rows = n_ctx * n_heads
    assert cache_k.shape == cache_v.shape == (nb, n_ctx, n_heads, dv) and d == n_heads * dv == 2 * dqk * n_heads
    assert rows % LANES == 0 and (n_heads * s_len) % LANES == 0 and s_len % BF16_SUBLANES == 0
    _log2(n_heads)
    frames = pl.BlockSpec((1, s_len, d), lambda b: (b, 0, 0))
    cache = pl.BlockSpec((1, rows, dv), lambda b: (b, 0, 0))
    return pl.pallas_call(
        functools.partial(_attn_cache_kernel, n_heads=n_heads, frame0=frame0, lam_init=lam_init, dqk=dqk),
        grid=(nb,),
        in_specs=[pl.BlockSpec((4, dqk), lambda b: (0, 0)), pl.BlockSpec((1, dv), lambda b: (0, 0)),
                  frames, cache, cache, frames, frames],
        out_specs=frames,
        out_shape=jax.ShapeDtypeStruct((nb, s_len, d), BF16),
        compiler_params=pltpu.CompilerParams(dimension_semantics=("parallel",), vmem_limit_bytes=VMEM_LIMIT),
        name="diff_attention_cache",
    )(lqk, subln_g, q, cache_k.reshape(nb, rows, dv), cache_v.reshape(nb, rows, dv), k_new, v_new)


def _attn_t_kernel(lqk_ref, g_ref, mbias_ref, dbias_ref, qt_ref, mk_ref, mvt_ref, kf_ref, vt_ref, o_ref,
                   *, tq, n_tiles, lam_init, dqk):
    lqk = lqk_ref[...]
    lam = (jnp.exp(jnp.sum(lqk[0:1] * lqk[1:2], axis=-1, keepdims=True))
           - jnp.exp(jnp.sum(lqk[2:3] * lqk[3:4], axis=-1, keepdims=True)) + lam_init)

    dv = qt_ref.shape[2]
    row = lax.broadcasted_iota(jnp.int32, (dv, tq), 0)
    zero = jnp.zeros((dv, tq), BF16)
    ones = jnp.ones((BF16_SUBLANES, tq), BF16)
    meta_fill = jnp.zeros((LANES - mk_ref.shape[0], 2 * tq), BF16)

    def query_tile(q):
        n = q + 2
        qt = qt_ref[0, q]
        w = jnp.concatenate([jnp.where(row < dqk, qt, zero), jnp.where(row >= dqk, qt, zero)], axis=1)

        def scores(t):
            if t == 0:
                return jnp.dot(mk_ref[...], w, preferred_element_type=F32) + mbias_ref[...]
            s = jnp.dot(kf_ref[0, (t - 1) * tq:t * tq, :], w, preferred_element_type=F32)
            return s + dbias_ref[...] if t == n - 1 else s

        s = scores(0)
        for t in range(n):
            s_next = scores(t + 1) if t + 1 < n else None
            s_max = jnp.max(s, axis=0, keepdims=True)
            if t == 0:
                m = s_max
                p = jnp.concatenate([jnp.exp2(s - m).astype(BF16), meta_fill], axis=0)
                acc = jnp.dot(jnp.concatenate([mvt_ref[...], ones[:, :LANES]], axis=0), p,
                              preferred_element_type=F32)
            else:
                m_new = jnp.maximum(m, s_max)
                acc = jnp.exp2(m - m_new) * acc + jnp.dot(
                    jnp.concatenate([vt_ref[0, t - 1], ones], axis=0), jnp.exp2(s - m_new).astype(BF16),
                    preferred_element_type=F32)
                m = m_new
            s = s_next

        l = acc[dv:dv + 1]
        ot = acc[:dv, :tq] / l[:, :tq] - lam * (acc[:dv, tq:] / l[:, tq:])
        ot = ot * lax.rsqrt(jnp.mean(ot * ot, axis=0, keepdims=True) + RMS_EPS)
        o_ref[0, q * tq:(q + 1) * tq, :] = (ot.T * g_ref[...] * (1.0 - lam_init)).astype(o_ref.dtype)

    for q in range(n_tiles):
        query_tile(q)


def _attention_bias(tq, n_meta, meta_rows):
    r = jnp.arange(tq, dtype=jnp.int32)[:, None]
    c = jnp.arange(2 * tq, dtype=jnp.int32)[None, :] % tq
    meta = jnp.broadcast_to(r[:meta_rows] < n_meta, (meta_rows, 2 * tq))
    diag = (r // CHUNK) <= (c // CHUNK)
    neg = jnp.float32(-jnp.inf)
    return jnp.where(meta, 0.0, neg), jnp.where(diag, 0.0, neg)


def _attention_t(lqk, subln_g, qt, mk, mvt, kf, vt, *, n_heads, n_meta, lam_init):
    nb, n_tiles, d, tq = qt.shape
    length = n_tiles * tq
    dv = d // n_heads
    dqk = lqk.shape[1]
    meta_rows = mk.shape[0]
    assert dv == 2 * dqk and kf.shape == (nb, length, d) and vt.shape == qt.shape
    assert mk.shape[1] == d and mvt.shape == (d, LANES) and n_meta <= meta_rows <= LANES and tq % CHUNK == 0
    return pl.pallas_call(
        functools.partial(_attn_t_kernel, tq=tq, n_tiles=n_tiles, lam_init=lam_init, dqk=dqk),
        grid=(nb, n_heads),
        in_specs=[pl.BlockSpec((4, dqk), lambda b, h: (0, 0)),
                  pl.BlockSpec((1, dv), lambda b, h: (0, 0)),
                  _const_spec((meta_rows, 2 * tq)), _const_spec((tq, 2 * tq)),
                  pl.BlockSpec((1, n_tiles, dv, tq), lambda b, h: (b, 0, h, 0)),
                  pl.BlockSpec((meta_rows, dv), lambda b, h: (0, h)),
                  pl.BlockSpec((dv, LANES), lambda b, h: (h, 0)),
                  pl.BlockSpec((1, length, dv), lambda b, h: (b, 0, h)),
                  pl.BlockSpec((1, n_tiles, dv, tq), lambda b, h: (b, 0, h, 0))],
        out_specs=pl.BlockSpec((1, length, dv), lambda b, h: (b, 0, h)),
        out_shape=jax.ShapeDtypeStruct((nb, length, d), BF16),
        compiler_params=pltpu.CompilerParams(
            dimension_semantics=("parallel", "parallel"), vmem_limit_bytes=VMEM_LIMIT),
        name="diff_attention_t",
    )(lqk, subln_g, *_attention_bias(tq, n_meta, meta_rows), qt, mk, mvt, kf, vt)


def _tail_kernel(x_ref, mc_ref, sga_ref, ya_ref, wb1_ref, wout_ref, g_ref, b_ref, wrt_ref, brt_ref,
                 h32_ref, route_ref, counts_ref, *, alpha, n_experts, n_groups, n_sub):
    @pl.when(pl.program_id(0) == 0)
    def _():
        counts_ref[...] = jnp.zeros_like(counts_ref)

    tm = x_ref.shape[0]
    sub = tm // n_sub
    lane = lax.broadcasted_iota(jnp.int32, (sub, ROUTE_LANES), 1)
    big = jnp.int32(ROUTE_LANES)
    neg = -jnp.inf
    per_group = n_experts // n_groups
    r = lax.broadcasted_iota(jnp.int32, (sub, sub), 0)
    c = lax.broadcasted_iota(jnp.int32, (sub, sub), 1)
    earlier = jnp.where(c < r, 1.0, 0.0).astype(BF16)

    def first_max(v):
        vmax = jnp.max(v, axis=-1, keepdims=True)
        return vmax, jnp.min(jnp.where(v == vmax, lane, big), axis=-1, keepdims=True)

    counts = counts_ref[...]
    for g in range(n_sub):
        rows = slice(g * sub, (g + 1) * sub)
        merged = mc_ref[rows].astype(F32) + sga_ref[rows].astype(F32) * jnp.dot(
            ya_ref[rows], wb1_ref[...], preferred_element_type=F32)
        y = alpha * x_ref[rows] + jnp.dot(merged.astype(BF16), wout_ref[...], preferred_element_type=F32)
        h = _layer_norm(y, g_ref[...], b_ref[...])
        h32_ref[rows] = h

        h_hi = h.astype(BF16)
        h_lo = (h - h_hi.astype(F32)).astype(BF16)
        hw = jnp.dot(h_hi, wrt_ref[...], preferred_element_type=F32)
        logits = (hw[:, :ROUTE_LANES] + hw[:, ROUTE_LANES:]
                  + jnp.dot(h_lo, wrt_ref[:, :ROUTE_LANES], preferred_element_type=F32)) + brt_ref[...]

        is_group = (lane >= n_experts) & (lane < n_experts + n_groups)
        lg = jnp.where(is_group, logits, neg)
        eg = jnp.exp(lg - jnp.max(lg, axis=-1, keepdims=True))
        p_group = jnp.where(is_group, eg / jnp.sum(eg, axis=-1, keepdims=True), neg)
        p_g, g_lane = first_max(p_group)
        g_idx = g_lane - n_experts
        lo = g_idx * per_group
        le = jnp.where((lane >= lo) & (lane < lo + per_group), logits, neg)
        v1, i1 = first_max(le)
        v2, i2 = first_max(jnp.where(lane == i1, neg, le))
        e2 = jnp.exp(v2 - v1)
        w1 = p_g / (1.0 + e2)
        w2 = p_g * e2 / (1.0 + e2)

        onehot = jnp.where((lane == i1) | (lane == i2), 1.0, 0.0).astype(BF16)
        before = jnp.dot(earlier, onehot, preferred_element_type=F32) + counts
        r1 = jnp.sum(jnp.where(lane == i1, before, 0.0), axis=-1, keepdims=True)
        r2 = jnp.sum(jnp.where(lane == i2, before, 0.0), axis=-1, keepdims=True)
        counts = counts + jnp.sum(onehot.astype(F32), axis=0, keepdims=True)

        route = jnp.zeros_like(logits)
        for k, col in ((ROUTE_E0, i1.astype(F32)), (ROUTE_E1, i2.astype(F32)), (ROUTE_W0, w1), (ROUTE_W1, w2),
                       (ROUTE_R0, r1), (ROUTE_R1, r2)):
            route = jnp.where(lane == k, col, route)
        route_ref[rows] = route
    counts_ref[...] = counts


def _tail(x, mc, sga, ya, wb1, w_out, ln_g, ln_b, w_rt, b_rt, *, tm, alpha, n_experts, n_groups):
    t, d = x.shape
    assert t % tm == 0
    n_sub = tm // TAIL_GROUP if tm % TAIL_GROUP == 0 else 1
    row = lambda w: pl.BlockSpec((tm, w), lambda i: (i, 0))
    return pl.pallas_call(
        functools.partial(_tail_kernel, alpha=alpha, n_experts=n_experts, n_groups=n_groups, n_sub=n_sub),
        grid=(t // tm,),
        in_specs=[row(d), row(d), row(d), row(d), _const_spec((d, d)), _const_spec((d, d)),
                  _const_spec((1, d)), _const_spec((1, d)),
                  _const_spec((d, 2 * ROUTE_LANES)), _const_spec((1, ROUTE_LANES))],
        out_specs=[row(d), row(ROUTE_LANES), pl.BlockSpec((1, ROUTE_LANES), lambda i: (0, 0))],
        out_shape=[jax.ShapeDtypeStruct((t, d), F32), jax.ShapeDtypeStruct((t, ROUTE_LANES), F32),
                   jax.ShapeDtypeStruct((1, ROUTE_LANES), F32)],
        compiler_params=pltpu.CompilerParams(
            dimension_semantics=("arbitrary",), vmem_limit_bytes=VMEM_LIMIT),
        name="merge_ln_router",
    )(x, mc, sga, ya, wb1, w_out, ln_g, ln_b, w_rt, b_rt)


def _row_copy(src_ref, src_row, dst_ref, dst_row, sem):
    return pltpu.make_async_copy(src_ref.at[pl.ds(src_row, 1)], dst_ref.at[pl.ds(dst_row, 1)], sem)


def _moe_scatter_kernel(pos_ref, pad_rows_ref, n_pad_ref, n_used_ref, h_ref, xs_ref, zero_ref, stage_ref, sems,
                        pad_sem, *, tm):
    @pl.when(pl.program_id(0) == 0)
    def _():
        zero_ref[...] = jnp.zeros_like(zero_ref)
        tr = zero_ref.shape[0]
        n_tiles = xs_ref.shape[0] // tr

        def tile_copy(r):
            return pltpu.make_async_copy(zero_ref, xs_ref.at[pl.ds(pl.multiple_of(r * tr, tr), tr)], pad_sem)

        def issue_pad(j, _):
            _row_copy(zero_ref, 0, xs_ref, pad_rows_ref[j], pad_sem).start()
            return 0

        def wait_pad(j, _):
            _row_copy(zero_ref, 0, xs_ref, 0, pad_sem).wait()
            return 0

        lax.fori_loop(0, n_pad_ref[0], issue_pad, 0)
        lax.fori_loop(n_used_ref[0], n_tiles, lambda r, _: (tile_copy(r).start(), 0)[1], 0)
        lax.fori_loop(0, n_pad_ref[0], wait_pad, 0)
        lax.fori_loop(n_used_ref[0], n_tiles, lambda r, _: (tile_copy(r).wait(), 0)[1], 0)

    i = pl.program_id(0)
    slot = i % 2
    stage_ref[slot] = h_ref[...]

    def issue(t, _):
        for k in range(TOP_K):
            _row_copy(stage_ref.at[slot], t, xs_ref, pos_ref[0, 0, k * tm + t], sems.at[slot]).start()
        return 0

    def wait_slot(s):
        for _ in range(TOP_K):
            pltpu.make_async_copy(stage_ref.at[s], xs_ref.at[pl.ds(0, tm)], sems.at[s]).wait()

    lax.fori_loop(0, tm, issue, 0, unroll=ROW_DMA_UNROLL)
    pl.when(i > 0)(lambda: wait_slot(1 - slot))
    pl.when(i == pl.num_programs(0) - 1)(lambda: wait_slot(slot))


def _moe_scatter(pos, pad_rows, n_pad, n_used, h32, n_rows, *, tm, tr):
    t, w = h32.shape
    smem = pl.BlockSpec(memory_space=pltpu.SMEM)
    return pl.pallas_call(
        functools.partial(_moe_scatter_kernel, tm=tm),
        grid=(t // tm,),
        in_specs=[pl.BlockSpec((1, 1, TOP_K * tm), lambda i: (i, 0, 0), memory_space=pltpu.SMEM),
                  smem, smem, smem, pl.BlockSpec((tm, w), lambda i: (i, 0))],
        out_specs=pl.BlockSpec(memory_space=pl.ANY),
        out_shape=jax.ShapeDtypeStruct((n_rows, w), h32.dtype),
        scratch_shapes=[pltpu.VMEM((tr, w), h32.dtype), pltpu.VMEM((2, tm, w), h32.dtype),
                        pltpu.SemaphoreType.DMA((2,)), pltpu.SemaphoreType.DMA(())],
        compiler_params=pltpu.CompilerParams(
            dimension_semantics=("arbitrary",), vmem_limit_bytes=VMEM_LIMIT, has_side_effects=True),
        name="moe_scatter",
    )(pos, pad_rows, n_pad, n_used, h32)


def _moe_experts_kernel(tile_expert_ref, n_used_ref, xs_ref, wgu_ref, wd_ref, ys_ref, *, d_expert):
    del tile_expert_ref
    used = pl.program_id(0) < n_used_ref[0]

    @pl.when(used)
    def _():
        ab = jnp.dot(xs_ref[...].astype(BF16), wgu_ref[0].astype(BF16), preferred_element_type=F32)
        a = ab[:, :d_expert]
        act = (a * jax.nn.sigmoid(a) * ab[:, d_expert:]).astype(BF16)
        ys_ref[...] = jnp.dot(act, wd_ref[0].astype(BF16), preferred_element_type=F32)

    @pl.when(jnp.logical_not(used))
    def _():
        ys_ref[...] = jnp.zeros_like(ys_ref)


def _moe_experts(tile_expert, n_used, xs, wgu, wd, *, tr):
    n_rows, w = xs.shape
    _, d, two_de = wgu.shape
    assert d == w
    last_used = lambda r, nu: jnp.minimum(r, nu[0] - 1)
    return pl.pallas_call(
        functools.partial(_moe_experts_kernel, d_expert=two_de // 2),
        grid_spec=pltpu.PrefetchScalarGridSpec(
            num_scalar_prefetch=2,
            grid=(n_rows // tr,),
            in_specs=[pl.BlockSpec((tr, w), lambda r, te, nu: (last_used(r, nu), 0)),
                      pl.BlockSpec((1, d, two_de), lambda r, te, nu: (te[r], 0, 0)),
                      pl.BlockSpec((1, two_de // 2, d), lambda r, te, nu: (te[r], 0, 0))],
            out_specs=pl.BlockSpec((tr, w), lambda r, te, nu: (r, 0))),
        out_shape=jax.ShapeDtypeStruct((n_rows, w), xs.dtype),
        compiler_params=pltpu.CompilerParams(
            dimension_semantics=("arbitrary",), vmem_limit_bytes=VMEM_LIMIT),
        name="moe_experts",
    )(tile_expert, n_used, xs, wgu, wd)


def _moe_combine_kernel(pos_ref, next_pos_ref, route_ref, h_ref, g_ref, b_ref, ys_ref, o_ref, y_buf, sems,
                        *, tm, alpha):
    i = pl.program_id(0)
    slot = i % 2

    def gather(p_ref, s):
        def issue(t, _):
            for k in range(TOP_K):
                _row_copy(ys_ref, p_ref[0, 0, k * tm + t], y_buf.at[s, k], t, sems.at[s]).start()
            return 0

        lax.fori_loop(0, tm, issue, 0, unroll=ROW_DMA_UNROLL)

    @pl.when(i == 0)
    def _():
        gather(pos_ref, slot)

    @pl.when(i + 1 < pl.num_programs(0))
    def _():
        gather(next_pos_ref, 1 - slot)

    for k in range(TOP_K):
        pltpu.make_async_copy(ys_ref.at[pl.ds(0, tm)], y_buf.at[slot, k], sems.at[slot]).wait()

    route = route_ref[...]
    lane = lax.broadcasted_iota(jnp.int32, route.shape, 1)
    f = jnp.zeros_like(h_ref)
    for k, w_lane in enumerate((ROUTE_W0, ROUTE_W1)):
        gate = jnp.sum(jnp.where(lane == w_lane, route, 0.0), axis=-1, keepdims=True)
        f = f + gate * y_buf[slot, k]
    o_ref[...] = _layer_norm(alpha * h_ref[...] + f, g_ref[...], b_ref[...])


def _moe_combine(pos, route, h32, ys, ln_g, ln_b, *, tm, alpha):
    t, d = h32.shape
    last = t // tm - 1
    pos_blk = lambda step: pl.BlockSpec((1, 1, TOP_K * tm), lambda i: (step(i), 0, 0), memory_space=pltpu.SMEM)
    return pl.pallas_call(
        functools.partial(_moe_combine_kernel, tm=tm, alpha=alpha),
        grid=(t // tm,),
        in_specs=[pos_blk(lambda i: i), pos_blk(lambda i: jnp.minimum(i + 1, last)),
                  pl.BlockSpec((tm, ROUTE_LANES), lambda i: (i, 0)),
                  pl.BlockSpec((tm, d), lambda i: (i, 0)),
                  _const_spec((1, d)), _const_spec((1, d)),
                  pl.BlockSpec(memory_space=pl.ANY)],
        out_specs=pl.BlockSpec((tm, d), lambda i: (i, 0)),
        out_shape=jax.ShapeDtypeStruct((t, d), F32),
        scratch_shapes=[pltpu.VMEM((2, TOP_K, tm) + ys.shape[1:], ys.dtype), pltpu.SemaphoreType.DMA((2,))],
        compiler_params=pltpu.CompilerParams(
            dimension_semantics=("arbitrary",), vmem_limit_bytes=VMEM_LIMIT),
        name="moe_combine",
    )(pos, pos, route, h32, ln_g, ln_b, ys)


def _moe(h32, route, counts, wgu, wd, ln_g, ln_b, *, tm, alpha):
    t, d = h32.shape
    n_experts = wgu.shape[0]
    tr = min(MOE_ROW_TILE, max(LANES // 2, TOP_K * t // n_experts))
    assert t % tm == 0
    n_tiles = (TOP_K * t + n_experts * (tr - 1)) // tr

    def segment_of(i, seg_ends):
        return jnp.minimum(jnp.sum(i[:, None] >= seg_ends[None, :], axis=1), n_experts - 1).astype(jnp.int32)

    experts = jnp.arange(n_experts, dtype=jnp.int32)
    cnt = counts[0, :n_experts].astype(jnp.int32)
    padded = (cnt + tr - 1) // tr * tr
    ends = jnp.cumsum(padded)
    starts = ends - padded
    expert = route[:, ROUTE_E0:ROUTE_E1 + 1].astype(jnp.int32)
    rank = route[:, ROUTE_R0:ROUTE_R1 + 1].astype(jnp.int32)
    pos = jnp.sum(jnp.where(expert[..., None] == experts, starts, 0), axis=-1) + rank
    pos = pos.reshape(t // tm, tm, TOP_K).transpose(0, 2, 1).reshape(t // tm, 1, TOP_K * tm)
    tile_expert = segment_of(jnp.arange(n_tiles, dtype=jnp.int32) * tr, ends)
    n_used = ends[-1:] // tr
    pad_ends = jnp.cumsum(padded - cnt)
    i = jnp.arange(n_experts * (tr - 1), dtype=jnp.int32)
    seg = segment_of(i, pad_ends)
    pad_rows = jnp.minimum((starts + cnt)[seg] + i - (pad_ends - (padded - cnt))[seg], n_tiles * tr - 1)

    xs = _moe_scatter(pos, pad_rows, pad_ends[-1:], n_used, h32, n_tiles * tr, tm=tm, tr=tr)
    ys = _moe_experts(tile_expert, n_used, xs, wgu, wd, tr=tr)
    return _moe_combine(pos, route, h32, ys, ln_g, ln_b, tm=tm, alpha=alpha)


def _pick_tile(n, target):
    t = min(n, target)
    while n % t:
        t //= 2
    return t


def kernel(x_prompt, x_sample, cache_k, cache_v, state_conv, meta_tokens, w_in, conv_w, lambda_qk, subln_g, w_branch, w_out, ln1_g, ln1_b, w_group, b_group, w_router, b_router, w_gate_up, w_down, ln2_g, ln2_b):
    depth = w_in.shape[0]
    assert depth == 1, "single-layer step only"
    bp, seq, d = x_prompt.shape
    bs, s_len, _ = x_sample.shape
    n_meta = meta_tokens.shape[0]
    n_heads = cache_k.shape[3]
    dqk = cache_k.shape[4] // 2
    past = cache_k.shape[2] - n_meta
    n_groups = w_group.shape[-1]
    n_experts = w_router.shape[-1]
    assert n_experts + n_groups <= ROUTE_LANES
    q_scale = dqk ** -0.5
    assert math.frexp(q_scale)[0] == 0.5, "the score scale is folded into q; exact only for powers of two"
    alpha = (2.0 * depth) ** 0.25
    lam_init = 0.8 - 0.6 * math.exp(-0.3 * 0)

    w_in_b = w_in[0].astype(BF16)
    wb = w_branch[0].astype(BF16)
    w_out_b = w_out[0].astype(BF16)
    w_rt = jnp.zeros((d, ROUTE_LANES), F32).at[:, :n_experts].set(w_router[0]).at[
        :, n_experts:n_experts + n_groups].set(w_group[0])
    w_rt_hi = w_rt.astype(BF16)
    w_rt = jnp.concatenate([w_rt_hi, (w_rt - w_rt_hi.astype(F32)).astype(BF16)], axis=1)
    b_rt = jnp.zeros((1, ROUTE_LANES), F32).at[0, :n_experts].set(b_router[0]).at[
        0, n_experts:n_experts + n_groups].set(b_group[0])

    def hist_rows(rows):
        return jnp.pad(rows, ((0, 0), (HIST - rows.shape[1], 0), (0, 0)))

    proj = functools.partial(_proj_conv, w_in=w_in_b, conv_w=conv_w[0], wb0=wb[0], wqt=w_in_b[:, 3 * d:4 * d].T,
                             q_scale=q_scale)
    tail = functools.partial(_tail, wb1=wb[1], w_out=w_out_b, ln_g=ln1_g, ln_b=ln1_b, w_rt=w_rt, b_rt=b_rt,
                             alpha=alpha, n_experts=n_experts, n_groups=n_groups)
    moe = functools.partial(_moe, wgu=w_gate_up[0], wd=w_down[0], ln_g=ln2_g, ln_b=ln2_b, alpha=alpha)

    mk32, mv32, _, mkb, mvb, _, _, mnc = proj(
        meta_tokens[None], jnp.zeros((1, HIST, d), F32), bb=1, ts=n_meta, transposed=False)
    ta = _pick_tile(seq, ATTN_TILE)
    assert n_meta <= LANES
    mk_pad = jnp.pad(mkb[0], ((0, -n_meta % BF16_SUBLANES), (0, 0)))
    mvt_pad = jnp.pad(mvb[0].T, ((0, 0), (0, LANES - n_meta)))
    k32, v32, qt, kb, vt, mc, sga, nc = proj(
        x_prompt, jnp.broadcast_to(mnc, (bp, HIST, d)), bb=1, ts=_pick_tile(seq, PROJ_TILE), transposed=True, ta=ta)
    ya = _attention_t(lambda_qk[0], subln_g, qt, mk_pad, mvt_pad, kb, vt,
                      n_heads=n_heads, n_meta=n_meta, lam_init=lam_init)
    t_p = bp * seq
    tm = _pick_tile(t_p, TAIL_TILE)
    routed = tail(x_prompt.reshape(t_p, d), mc.reshape(t_p, d), sga.reshape(t_p, d), ya.reshape(t_p, d), tm=tm)
    y_prompt = moe(*routed, tm=_pick_tile(t_p, MOE_TOKEN_TILE)).reshape(bp, seq, d)

    sk32, sv32, sqb, skb, svb, smc, ssga, snc = proj(
        x_sample, hist_rows(state_conv[0]), bb=bs, ts=s_len, transposed=False)
    sya = _attention_cache(lambda_qk[0], subln_g, sqb, cache_k[0], cache_v[0], skb, svb,
                           frame0=past, lam_init=lam_init)
    t_s = bs * s_len
    srouted = tail(x_sample.reshape(t_s, d), smc.reshape(t_s, d), ssga.reshape(t_s, d), sya.reshape(t_s, d),
                   tm=t_s)
    y_sample = moe(*srouted, tm=t_s).reshape(bs, s_len, d)

    def with_meta(m, f):
        full = jnp.concatenate([jnp.broadcast_to(m, (bp, n_meta, d)), f], axis=1)
        return full.reshape(1, bp, n_meta + seq, n_heads, d // n_heads)

    return (y_prompt, y_sample,
            with_meta(mk32, k32), with_meta(mv32, v32), nc[None, :, HIST - 2:],
            sk32.reshape(1, bs, s_len, n_heads, d // n_heads),
            sv32.reshape(1, bs, s_len, n_heads, d // n_heads), snc[None, :, HIST - 2:])
```
